```python
import jax
import jax.numpy as jnp
from jax import lax
import numpy as np

D_MODEL = 1024
BATCH = 8
SEQ = 4096
DEPTH = 1
DEC_BATCH = 128
DEC_SEQ = 4
PAST_LEN = 8192
PAGE_SIZE = 128

MEM_LEN = 256
A_HEADS = 8
A_DIM = 64
ROT_DIM = A_DIM // 4
ROPE_THETA = 500000.0
MOBA_BLOCK = 256
MOBA_TOPK = 3
MOBA_QCHUNK = 32
G_HEADS = 4
G_DK = 32
G_DV = 64
G_LOWRANK = 16
G_TAU = 16.0
G_CHUNK = 64
M_HEADS = 4
M_DIM = 64
A_W = A_HEADS * A_DIM
G_K = G_HEADS * G_DK
G_V = G_HEADS * G_DV
M_W = M_HEADS * M_DIM
MIX_W = A_W + G_V + M_W
IN_W = 3 * A_W + 2 * G_K + 2 * G_V + G_LOWRANK + M_W
N_EXPERTS = 32
TOP_K = 4
D_FF = 1024
SWIGLU_ALPHA = 1.702
SWIGLU_LIMIT = 7.0
MOE_BLOCK = 128
LN_EPS = 1e-5
RMS_EPS = 1e-6
NEG = -1e30
DEEPNORM_ALPHA = (2 * DEPTH) ** 0.25
DEEPNORM_BETA = (8 * DEPTH) ** -0.25
F32 = jnp.float32

kernel_name = 'moba_gla_memory_moe_hybrid_step'


def layer_norm(x, g, b):
    xf = x.astype(F32)
    mu = jnp.mean(xf, axis=-1, keepdims=True)
    var = jnp.mean(jnp.square(xf - mu), axis=-1, keepdims=True)
    return ((xf - mu) * lax.rsqrt(var + LN_EPS) * g.astype(F32) + b.astype(F32)).astype(x.dtype)


def rms_norm(x, g):
    xf = x.astype(F32)
    return xf * lax.rsqrt(jnp.mean(xf * xf, axis=-1, keepdims=True) + RMS_EPS) * g.astype(F32)


def rope_partial(x, pos):
    half = ROT_DIM // 2
    inv = jnp.power(ROPE_THETA, -jnp.arange(half, dtype=F32) / half)
    ang = pos.astype(F32)[:, None] * inv[None, :]
    cos = jnp.cos(ang)[None, :, None, :]
    sin = jnp.sin(ang)[None, :, None, :]
    xr = x[..., :ROT_DIM].astype(F32)
    x1, x2 = xr[..., :half], xr[..., half:]
    rot = jnp.concatenate([x1 * cos - x2 * sin, x2 * cos + x1 * sin], axis=-1).astype(x.dtype)
    return jnp.concatenate([rot, x[..., ROT_DIM:]], axis=-1)


def moba_prompt(q, k, v):
    B, S, H, dh = q.shape
    nblk = -(-S // MOBA_BLOCK)
    n_sel = min(MOBA_TOPK, nblk - 1)
    pad = nblk * MOBA_BLOCK - S
    kb = jnp.pad(k, ((0, 0), (0, pad), (0, 0), (0, 0))).reshape(B, nblk, MOBA_BLOCK, H, dh)
    vb = jnp.pad(v, ((0, 0), (0, pad), (0, 0), (0, 0))).reshape(B, nblk, MOBA_BLOCK, H, dh)
    k_mean = jnp.mean(kb.astype(F32), axis=2)
    scale = dh ** -0.5
    nq = S // MOBA_QCHUNK
    qs = q.reshape(B, nq, MOBA_QCHUNK, H, dh).transpose(1, 0, 2, 3, 4)
    bi = jnp.arange(B)[:, None, None, None]
    hi = jnp.arange(H)[None, None, :, None]

    def step(args):
        qc, c = args
        t0 = c * MOBA_QCHUNK
        bq = t0 // MOBA_BLOCK
        tpos = t0 + jnp.arange(MOBA_QCHUNK)
        k_own = lax.dynamic_index_in_dim(kb, bq, axis=1, keepdims=False)
        v_own = lax.dynamic_index_in_dim(vb, bq, axis=1, keepdims=False)
        own_ok = (bq * MOBA_BLOCK + jnp.arange(MOBA_BLOCK))[None, :] <= tpos[:, None]
        s_own = jnp.einsum('bqhd,blhd->bqhl', qc, k_own).astype(F32) * scale
        s_own = jnp.where(own_ok[None, :, None, :], s_own, NEG)
        if n_sel == 0:
            p = jax.nn.softmax(s_own, axis=-1).astype(v.dtype)
            return jnp.einsum('bqhl,blhd->bqhd', p, v_own)
        gate = jnp.einsum('bqhd,bnhd->bqhn', qc.astype(F32), k_mean)
        gate = jnp.where(jnp.arange(nblk) < bq, gate, NEG)
        _, idx = lax.top_k(gate, n_sel)
        ok = idx < bq
        k_sel = kb[bi, idx, :, hi]
        v_sel = vb[bi, idx, :, hi]
        s_sel = jnp.einsum('bqhd,bqhkld->bqhkl', qc, k_sel).astype(F32) * scale
        s_sel = jnp.where(ok[..., None], s_sel, NEG).reshape(B, MOBA_QCHUNK, H, n_sel * MOBA_BLOCK)
        p = jax.nn.softmax(jnp.concatenate([s_sel, s_own], axis=-1), axis=-1).astype(v.dtype)
        p_sel = p[..., :n_sel * MOBA_BLOCK].reshape(B, MOBA_QCHUNK, H, n_sel, MOBA_BLOCK)
        return (jnp.einsum('bqhkl,bqhkld->bqhd', p_sel, v_sel)
                + jnp.einsum('bqhl,blhd->bqhd', p[..., n_sel * MOBA_BLOCK:], v_own))

    out = lax.map(step, (qs, jnp.arange(nq)))
    return out.transpose(1, 0, 2, 3, 4).reshape(B, S, H, dh)


def moba_sample(q, k_new, v_new, ck, cv, page_table):
    DB, T, H, dh = q.shape
    past = page_table.shape[1] * PAGE_SIZE
    ppb = MOBA_BLOCK // PAGE_SIZE
    b0 = past // MOBA_BLOCK
    n_sel = min(MOBA_TOPK, b0)
    n_own = (past - b0 * MOBA_BLOCK) // PAGE_SIZE
    scale = dh ** -0.5
    k_own, v_own = k_new, v_new
    if n_own > 0:
        own_pt = page_table[:, b0 * ppb:b0 * ppb + n_own]
        k_own = jnp.concatenate([ck[own_pt].reshape(DB, n_own * PAGE_SIZE, H, dh), k_new], axis=1)
        v_own = jnp.concatenate([cv[own_pt].reshape(DB, n_own * PAGE_SIZE, H, dh), v_new], axis=1)
    r = n_own * PAGE_SIZE
    own_ok = jnp.concatenate([jnp.ones((T, r), bool), jnp.tril(jnp.ones((T, T), bool))], axis=1)
    s_own = jnp.einsum('bqhd,blhd->bqhl', q, k_own).astype(F32) * scale
    s_own = jnp.where(own_ok[None, :, None, :], s_own, NEG)
    if n_sel == 0:
        p = jax.nn.softmax(s_own, axis=-1).astype(v_own.dtype)
        return jnp.einsum('bqhl,blhd->bqhd', p, v_own)
    k_mean = jnp.mean(ck[page_table[:, :b0 * ppb]].astype(F32).reshape(DB, b0, MOBA_BLOCK, H, dh), axis=2)
    gate = jnp.einsum('bqhd,bnhd->bqhn', q.astype(F32), k_mean)
    _, idx = lax.top_k(gate, n_sel)
    bi = jnp.arange(DB)[:, None, None, None, None]
    hi = jnp.arange(H)[None, None, :, None, None]
    phys = page_table[bi, idx[..., None] * ppb + jnp.arange(ppb)]
    k_sel = ck[phys, :, hi].reshape(DB, T, H, n_sel, MOBA_BLOCK, dh)
    v_sel = cv[phys, :, hi].reshape(DB, T, H, n_sel, MOBA_BLOCK, dh)
    s_sel = jnp.einsum('bqhd,bqhkld->bqhkl', q, k_sel).astype(F32).reshape(DB, T, H, n_sel * MOBA_BLOCK) * scale
    p = jax.nn.softmax(jnp.concatenate([s_sel, s_own], axis=-1), axis=-1).astype(v_new.dtype)
    p_sel = p[..., :n_sel * MOBA_BLOCK].reshape(DB, T, H, n_sel, MOBA_BLOCK)
    return (jnp.einsum('bqhkl,bqhkld->bqhd', p_sel, v_sel)
            + jnp.einsum('bqhl,blhd->bqhd', p[..., n_sel * MOBA_BLOCK:], v_own))


def gla_chunked(q, k, v, log_a, s0):
    B, T, H, dk = q.shape
    c = G_CHUNK if T % G_CHUNK == 0 else T
    n = T // c
    causal = jnp.tril(jnp.ones((c, c), bool))

    def chunks(z):
        return z.astype(F32).reshape(B, n, c, *z.shape[2:]).swapaxes(0, 1)

    def step(s, xs):
        qc, kc, vc, ac = xs
        b = jnp.cumsum(ac, axis=1)
        o_inter = jnp.einsum('bihd,bhdv->bihv', qc * jnp.exp(b), s)
        diff = b[:, :, None] - b[:, None, :]
        decay = jnp.exp(jnp.where(causal[None, :, :, None, None], diff, NEG))
        att = jnp.einsum('bihd,bjhd,bijhd->bhij', qc, kc, decay)
        o_intra = jnp.einsum('bhij,bjhv->bihv', att, vc)
        b_last = b[:, -1]
        k_dec = kc * jnp.exp(b_last[:, None] - b)
        s_new = jnp.exp(b_last)[..., None] * s + jnp.einsum('bjhd,bjhv->bhdv', k_dec, vc)
        return s_new, o_inter + o_intra

    s_fin, o = lax.scan(step, s0.astype(F32), (chunks(q), chunks(k), chunks(v), chunks(log_a)))
    return o.swapaxes(0, 1).reshape(B, T, H, v.shape[-1]), s_fin


def mem_attention(q, mk, mv):
    s = jnp.einsum('bthd,bmhd->bthm', q, mk).astype(F32) * (M_DIM ** -0.5)
    p = jax.nn.softmax(s, axis=-1).astype(mv.dtype)
    return jnp.einsum('bthm,bmhd->bthd', p, mv)


def mem_kv(mem, w_mem_kv):
    B = mem.shape[0]
    mk, mv = jnp.split(mem @ w_mem_kv, 2, axis=-1)
    return mk.reshape(B, MEM_LEN, M_HEADS, M_DIM), mv.reshape(B, MEM_LEN, M_HEADS, M_DIM)


def moe(h, w_router, b_router, w_gate, b_gate, w_up, b_up, w_down, b_down):
    n, d = h.shape
    logits = (h @ w_router).astype(F32) + b_router.astype(F32)
    top_val, top_idx = lax.top_k(logits, TOP_K)
    gate = jax.nn.softmax(top_val, axis=-1)
    flat_e = top_idx.reshape(-1)
    order = jnp.argsort(flat_e)
    sorted_e = flat_e[order]
    counts = jnp.bincount(flat_e, length=N_EXPERTS)
    start = jnp.cumsum(counts) - counts
    padded = (counts + MOE_BLOCK - 1) // MOE_BLOCK * MOE_BLOCK
    pad_end = jnp.cumsum(padded)
    pad_start = pad_end - padded
    dest = (pad_start[sorted_e] + jnp.arange(n * TOP_K) - start[sorted_e]).astype(jnp.int32)
    n_blocks = -(-(n * TOP_K) // MOE_BLOCK) + N_EXPERTS
    row_tok = jnp.full((n_blocks * MOE_BLOCK,), n, jnp.int32).at[dest].set((order // TOP_K).astype(jnp.int32))
    block_e = jnp.minimum(jnp.searchsorted(pad_end, jnp.arange(n_blocks) * MOE_BLOCK, side='right'), N_EXPERTS - 1)
    xb = jnp.concatenate([h, jnp.zeros((1, d), h.dtype)], axis=0)[row_tok].reshape(n_blocks, MOE_BLOCK, d)

    def expert_block(args):
        xe, e = args
        g = jnp.minimum(xe @ w_gate[e] + b_gate[e], SWIGLU_LIMIT)
        u = jnp.clip(xe @ w_up[e] + b_up[e], -SWIGLU_LIMIT, SWIGLU_LIMIT)
        return (g * jax.nn.sigmoid(SWIGLU_ALPHA * g) * (u + 1.0)) @ w_down[e] + b_down[e]

    yb = lax.map(expert_block, (xb, block_e)).reshape(n_blocks * MOE_BLOCK, d)
    slot = jnp.zeros((n * TOP_K,), jnp.int32).at[order].set(dest)
    return jnp.einsum('nk,nkd->nd', gate.astype(yb.dtype), yb[slot].reshape(n, TOP_K, d))


def trunk_layer(h, pos, moba_fn, gla_s0, mk, mv, w_in, w_gla_gate, b_gla_gate, g_gla, w_out,
                ln1_g, ln1_b, w_router, b_router, w_gate, b_gate, w_up, b_up, w_down, b_down, ln2_g, ln2_b):
    B, T, _ = h.shape
    p = h @ w_in
    split_at = np.cumsum((A_W, A_W, A_W, G_K, G_K, G_V, G_LOWRANK, G_V)).tolist()
    qa, ka, va, qg, kg, vg, lg, rg, qm = jnp.split(p, split_at, axis=-1)
    qa = rope_partial(qa.reshape(B, T, A_HEADS, A_DIM), pos)
    ka = rope_partial(ka.reshape(B, T, A_HEADS, A_DIM), pos)
    va = va.reshape(B, T, A_HEADS, A_DIM)
    o_a = moba_fn(qa, ka, va)
    log_a = jax.nn.log_sigmoid((lg @ w_gla_gate).astype(F32) + b_gla_gate.astype(F32)) / G_TAU
    o_g, s_g = gla_chunked(qg.reshape(B, T, G_HEADS, G_DK) * (G_DK ** -0.5), kg.reshape(B, T, G_HEADS, G_DK),
                           vg.reshape(B, T, G_HEADS, G_DV), log_a.reshape(B, T, G_HEADS, G_DK), gla_s0)
    o_g = (rms_norm(o_g, g_gla) * jax.nn.silu(rg.reshape(B, T, G_HEADS, G_DV).astype(F32))).astype(h.dtype)
    o_m = mem_attention(qm.reshape(B, T, M_HEADS, M_DIM), mk, mv)
    o = jnp.concatenate([o_a.reshape(B, T, A_W), o_g.reshape(B, T, G_V), o_m.reshape(B, T, M_W)], axis=-1) @ w_out
    h = layer_norm(DEEPNORM_ALPHA * h + o, ln1_g, ln1_b)
    f = moe(h.reshape(B * T, D_MODEL), w_router, b_router, w_gate, b_gate, w_up, b_up, w_down, b_down)
    h = layer_norm(DEEPNORM_ALPHA * h + f.reshape(B, T, D_MODEL), ln2_g, ln2_b)
    return h, ka, va, s_g


def setup_inputs(seed: int = 0) -> dict:
    key = jax.random.key(seed)
    keys = iter(jax.random.split(key, 32))
    n_pages = PAST_LEN // PAGE_SIZE
    n_used = DEC_BATCH * n_pages
    n_phys = n_used + -(-n_used // 4)

    def nrm(shape, scale=1.0):
        return jax.random.normal(next(keys), shape, F32) * scale

    return {
        'x_prompt': nrm((BATCH, SEQ, D_MODEL)),
        'x_sample': nrm((DEC_BATCH, DEC_SEQ, D_MODEL)),
        'cache_k': nrm((DEPTH, n_phys, PAGE_SIZE, A_HEADS, A_DIM)),
        'cache_v': nrm((DEPTH, n_phys, PAGE_SIZE, A_HEADS, A_DIM)),
        'page_table': jax.random.permutation(next(keys), n_phys)[:n_used].reshape(DEC_BATCH, n_pages).astype(jnp.int32),
        'state_gla': nrm((DEPTH, DEC_BATCH, G_HEADS, G_DK, G_DV), 0.5),
        'cache_mem_k': nrm((DEPTH, DEC_BATCH, MEM_LEN, M_HEADS, M_DIM)),
        'cache_mem_v': nrm((DEPTH, DEC_BATCH, MEM_LEN, M_HEADS, M_DIM)),
        'mem_prompt': nrm((BATCH, MEM_LEN, D_MODEL)),
        'ln_in_g': 1.0 + nrm((D_MODEL,), 0.02),
        'ln_in_b': nrm((D_MODEL,), 0.02),
        'w_in': nrm((DEPTH, D_MODEL, IN_W), D_MODEL ** -0.5),
        'w_gla_gate': nrm((DEPTH, G_LOWRANK, G_K), G_LOWRANK ** -0.5),
        'b_gla_gate': nrm((DEPTH, G_K), 0.1),
        'g_gla': 1.0 + nrm((DEPTH, G_DV), 0.02),
        'w_mem_kv': nrm((DEPTH, D_MODEL, 2 * M_W), D_MODEL ** -0.5),
        'w_out': nrm((DEPTH, MIX_W, D_MODEL), DEEPNORM_BETA * MIX_W ** -0.5),
        'ln1_g': 1.0 + nrm((DEPTH, D_MODEL), 0.02),
        'ln1_b': nrm((DEPTH, D_MODEL), 0.02),
        'w_router': nrm((DEPTH, D_MODEL, N_EXPERTS), D_MODEL ** -0.5),
        'b_router': nrm((DEPTH, N_EXPERTS), 0.01),
        'w_gate': nrm((DEPTH, N_EXPERTS, D_MODEL, D_FF), D_MODEL ** -0.5),
        'b_gate': nrm((DEPTH, N_EXPERTS, D_FF), 0.02),
        'w_up': nrm((DEPTH, N_EXPERTS, D_MODEL, D_FF), D_MODEL ** -0.5),
        'b_up': nrm((DEPTH, N_EXPERTS, D_FF), 0.02),
        'w_down': nrm((DEPTH, N_EXPERTS, D_FF, D_MODEL), DEEPNORM_BETA * D_FF ** -0.5),
        'b_down': nrm((DEPTH, N_EXPERTS, D_MODEL), 0.02),
        'ln2_g': 1.0 + nrm((DEPTH, D_MODEL), 0.02),
        'ln2_b': nrm((DEPTH, D_MODEL), 0.02),
    }


def reference(x_prompt, x_sample, cache_k, cache_v, page_table, state_gla, cache_mem_k, cache_mem_v, mem_prompt,
              ln_in_g, ln_in_b, w_in, w_gla_gate, b_gla_gate, g_gla, w_mem_kv, w_out, ln1_g, ln1_b,
              w_router, b_router, w_gate, b_gate, w_up, b_up, w_down, b_down, ln2_g, ln2_b):
    past = page_table.shape[1] * PAGE_SIZE
    hp = layer_norm(x_prompt, ln_in_g, ln_in_b)
    hs = layer_norm(x_sample, ln_in_g, ln_in_b)
    pos_p = jnp.arange(x_prompt.shape[1], dtype=jnp.int32)
    pos_s = past + jnp.arange(x_sample.shape[1], dtype=jnp.int32)
    kp_l, vp_l, gp_l, mkp_l, mvp_l, ks_l, vs_l, gs_l = [], [], [], [], [], [], [], []
    for l in range(DEPTH):
        lw = (w_in[l], w_gla_gate[l], b_gla_gate[l], g_gla[l], w_out[l], ln1_g[l], ln1_b[l],
              w_router[l], b_router[l], w_gate[l], b_gate[l], w_up[l], b_up[l], w_down[l], b_down[l],
              ln2_g[l], ln2_b[l])
        mk_p, mv_p = mem_kv(mem_prompt, w_mem_kv[l])
        s0 = jnp.zeros((x_prompt.shape[0], G_HEADS, G_DK, G_DV), F32)
        hp, kp, vp, gp = trunk_layer(hp, pos_p, moba_prompt, s0, mk_p, mv_p, *lw)
        moba_s = lambda q, k, v, l=l: moba_sample(q, k, v, cache_k[l], cache_v[l], page_table)
        hs, ks, vs, gs = trunk_layer(hs, pos_s, moba_s, state_gla[l], cache_mem_k[l], cache_mem_v[l], *lw)
        kp_l.append(kp)
        vp_l.append(vp)
        gp_l.append(gp.astype(state_gla.dtype))
        mkp_l.append(mk_p)
        mvp_l.append(mv_p)
        ks_l.append(ks)
        vs_l.append(vs)
        gs_l.append(gs.astype(state_gla.dtype))
    return (hp, hs, jnp.stack(kp_l), jnp.stack(vp_l), jnp.stack(gp_l), jnp.stack(mkp_l), jnp.stack(mvp_l),
            jnp.stack(ks_l), jnp.stack(vs_l), jnp.stack(gs_l))
```

```python
import functools

import jax
import jax.numpy as jnp
from jax import lax
from jax.experimental import pallas as pl
from jax.experimental.pallas import tpu as pltpu

F32 = jnp.float32
BF16 = jnp.bfloat16
HI = lax.Precision.HIGHEST
NT = (((1,), (1,)), ((), ()))

A_HEADS, A_DIM = 8, 64
ROT_DIM = A_DIM // 4
ROPE_THETA = 500000.0
MOBA_BLOCK, MOBA_TOPK = 256, 3
G_HEADS, G_DK, G_DV, G_LOWRANK, G_TAU = 4, 32, 64, 16, 16.0
M_HEADS, M_DIM = 4, 64
A_W = A_HEADS * A_DIM
G_K = G_HEADS * G_DK
G_V = G_HEADS * G_DV
M_W = M_HEADS * M_DIM
N_EXPERTS, TOP_K = 32, 4
SWIGLU_ALPHA, SWIGLU_LIMIT = 1.702, 7.0
LN_EPS, RMS_EPS = 1e-5, 1e-6
NEG = -1e30
DEPTH = 1
DEEPNORM_ALPHA = (2 * DEPTH) ** 0.25

LANES = 128
VMEM_LIMIT = 56 * 1024 * 1024

MOE_TM = 512
GLA_CHUNK = 16


def _cparams(sem):
    return pltpu.CompilerParams(dimension_semantics=sem, vmem_limit_bytes=VMEM_LIMIT)


def _layer_norm(x, g, b):
    mu = jnp.mean(x, axis=-1, keepdims=True)
    xc = x - mu
    var = jnp.mean(xc * xc, axis=-1, keepdims=True)
    return xc * lax.rsqrt(var + LN_EPS) * g + b


_C_QA, _C_KA, _C_VA = 0, A_W, 2 * A_W
_C_QG = 3 * A_W
_C_KG = _C_QG + G_K
_C_VG = _C_KG + G_K
_C_RG = _C_VG + G_V
_C_QM = _C_RG + G_V
_C_LG = _C_QM + M_W
_C_END = _C_LG + LANES


def _proj_kernel(x_ref, g_ref, b_ref, w_ref, wgg_ref, bgg_ref, c_ref, s1_ref, s2_ref,
                 q_ref, k_ref, v_ref, gq_ref, gk_ref, gv_ref, la_ref, rg_ref, qm_ref):
    h = _layer_norm(x_ref[0], g_ref[...], b_ref[...]).astype(BF16)

    def mm(lo, hi):
        return jnp.dot(h, w_ref[:, lo:hi], preferred_element_type=F32)

    c, s1, s2 = c_ref[...], s1_ref[...], s2_ref[...]
    half = ROT_DIM // 2

    def rope(t):
        return t * c + pltpu.roll(t, A_W - half, 1) * s1 + pltpu.roll(t, half, 1) * s2

    q_ref[0] = rope(mm(_C_QA, _C_KA)) * (A_DIM ** -0.5)
    k_ref[0] = rope(mm(_C_KA, _C_VA))
    v_ref[0] = mm(_C_VA, _C_QG)
    gq_ref[0] = mm(_C_QG, _C_KG)
    gk_ref[0] = mm(_C_KG, _C_VG)
    gv_ref[0] = mm(_C_VG, _C_RG)
    rg_ref[0] = mm(_C_RG, _C_QM)
    qm_ref[0] = mm(_C_QM, _C_LG) * (M_DIM ** -0.5)
    z = jnp.dot(mm(_C_LG, _C_END), wgg_ref[...], precision=HI, preferred_element_type=F32) + bgg_ref[...]
    la_ref[0] = (jnp.minimum(z, 0.0) - jnp.log(1.0 + jnp.exp(-jnp.abs(z)))) * (1.0 / G_TAU)


def _rope_tables(pos):
    half = ROT_DIM // 2
    inv = jnp.power(ROPE_THETA, -jnp.arange(half, dtype=F32) / half)
    ang = pos.astype(F32)[:, None] * inv[None, :]
    cos, sin = jnp.cos(ang), jnp.sin(ang)
    n = pos.shape[0]
    one = jnp.ones((n, A_DIM - ROT_DIM), F32)
    zero8 = jnp.zeros((n, half), F32)
    zero = jnp.zeros((n, A_DIM - ROT_DIM), F32)
    c = jnp.concatenate([cos, cos, one], axis=1)
    s1 = jnp.concatenate([-sin, zero8, zero], axis=1)
    s2 = jnp.concatenate([zero8, sin, zero], axis=1)
    return tuple(jnp.tile(t, (1, A_HEADS)) for t in (c, s1, s2))


def _project(x, pos_tables, ln_g, ln_b, w_perm, wgg, bgg, ts):
    bz, s, d = x.shape
    widths = (A_W, A_W, A_W, G_K, G_K, G_V, G_K, G_V, M_W)
    tok = lambda w: pl.BlockSpec((1, ts, w), lambda j, b: (b, j, 0))
    full = lambda a: pl.BlockSpec(a.shape, lambda j, b: (0,) * a.ndim)
    tab = pl.BlockSpec((ts, A_W), lambda j, b: (j, 0))
    return pl.pallas_call(
        _proj_kernel,
        grid=(s // ts, bz),
        in_specs=[tok(d), full(ln_g), full(ln_b), full(w_perm), full(wgg), full(bgg), tab, tab, tab],
        out_specs=[tok(w) for w in widths],
        out_shape=[jax.ShapeDtypeStruct((bz, s, w), F32) for w in widths],
        compiler_params=_cparams(("arbitrary", "arbitrary")),
        name="proj",
    )(x, ln_g, ln_b, w_perm, wgg, bgg, *pos_tables)


def _matmul_kernel(x_ref, w_ref, o_ref):
    o_ref[...] = jnp.dot(x_ref[...].astype(BF16), w_ref[...], preferred_element_type=F32)


def _matmul(x, w_bf16, tm):
    n, d = x.shape
    return pl.pallas_call(
        _matmul_kernel,
        grid=(n // tm,),
        in_specs=[pl.BlockSpec((tm, d), lambda i: (i, 0)), pl.BlockSpec(w_bf16.shape, lambda i: (0, 0))],
        out_specs=pl.BlockSpec((tm, w_bf16.shape[1]), lambda i: (i, 0)),
        out_shape=jax.ShapeDtypeStruct((n, w_bf16.shape[1]), F32),
        compiler_params=_cparams(("arbitrary",)),
        name="mem_kv",
    )(x, w_bf16)


def _moba_prompt_kernel(nblk, q_ref, k_ref, v_ref, o_ref, kb_scr, vt_scr, km_scr, selb_scr):
    blk = MOBA_BLOCK
    i = pl.program_id(2)

    @pl.when(i == 0)
    def _():
        def prep(j, c):
            kj = k_ref[0, pl.ds(pl.multiple_of(j * blk, blk), blk), :]
            kb_scr[j] = kj.astype(BF16)
            km_scr[pl.ds(j, 1), :] = jnp.mean(kj, axis=0, keepdims=True)
            vj = v_ref[0, pl.ds(pl.multiple_of(j * blk, blk), blk), :]
            vt_scr[j] = vj.T.astype(BF16)
            return c
        lax.fori_loop(0, nblk, prep, 0)

    q_t = q_ref[0].T
    row = lax.broadcasted_iota(jnp.int32, q_t.shape, 0)
    blk_iota = lax.broadcasted_iota(jnp.int32, (nblk, blk), 0)
    km = km_scr[...]
    q_tb = []
    for h in range(2):
        q_h = jnp.where((row >= A_DIM * h) & (row < A_DIM * (h + 1)), q_t, 0.0)
        q_tb.append(q_h.astype(BF16))
        g = jnp.dot(km, q_h, precision=HI, preferred_element_type=F32)
        cnt = jnp.zeros((nblk, blk), jnp.int32)
        for jp in range(nblk):
            gj = g[jp:jp + 1, :]
            beats = (gj > g) | ((gj == g) & (jp < blk_iota))
            cnt = cnt + jnp.where(beats, 1, 0) * (jp < i).astype(jnp.int32)
        selb_scr[h] = jnp.where((blk_iota < i) & (cnt < MOBA_TOPK), 0.0, NEG)

    kpos = lax.broadcasted_iota(jnp.int32, (blk, blk), 0)
    qpos = lax.broadcasted_iota(jnp.int32, (blk, blk), 1)
    k_own, vt_own = kb_scr[i], vt_scr[i]
    state = []
    for h in range(2):
        s = jnp.dot(k_own, q_tb[h], preferred_element_type=F32)
        s = jnp.where(kpos <= qpos, s, NEG)
        m = jnp.max(s, axis=0, keepdims=True)
        p = jnp.exp(s - m)
        l = jnp.sum(p, axis=0, keepdims=True)
        acc = jnp.dot(vt_own[A_DIM * h:A_DIM * (h + 1), :], p.astype(BF16), preferred_element_type=F32)
        state += [m, l, acc]

    def past(j, carry):
        kj, vtj = kb_scr[j], vt_scr[j]
        new = []
        for h in range(2):
            m, l, acc = carry[3 * h:3 * h + 3]
            s = jnp.dot(kj, q_tb[h], preferred_element_type=F32) + selb_scr[h, pl.ds(j, 1), :]
            m_new = jnp.maximum(m, jnp.max(s, axis=0, keepdims=True))
            alpha = jnp.exp(m - m_new)
            p = jnp.exp(s - m_new)
            l = alpha * l + jnp.sum(p, axis=0, keepdims=True)
            acc = alpha * acc + jnp.dot(vtj[A_DIM * h:A_DIM * (h + 1), :], p.astype(BF16),
                                        preferred_element_type=F32)
            new += [m_new, l, acc]
        return tuple(new)

    m0, l0, acc0, m1, l1, acc1 = lax.fori_loop(0, i, past, tuple(state))
    o_t = jnp.concatenate([acc0 / l0, acc1 / l1], axis=0)
    o_ref[0] = o_t.T


def _moba_prompt(q, k, v):
    bz, s, _ = q.shape
    blk = MOBA_BLOCK
    nblk = s // blk
    pair = 2 * A_DIM
    return pl.pallas_call(
        functools.partial(_moba_prompt_kernel, nblk),
        grid=(bz, A_W // pair, nblk),
        in_specs=[pl.BlockSpec((1, blk, pair), lambda b, hp, i: (b, i, hp)),
                  pl.BlockSpec((1, s, pair), lambda b, hp, i: (b, 0, hp)),
                  pl.BlockSpec((1, s, pair), lambda b, hp, i: (b, 0, hp))],
        out_specs=pl.BlockSpec((1, blk, pair), lambda b, hp, i: (b, i, hp)),
        out_shape=jax.ShapeDtypeStruct((bz, s, A_W), F32),
        scratch_shapes=[pltpu.VMEM((nblk, blk, pair), BF16), pltpu.VMEM((nblk, pair, blk), BF16),
                        pltpu.VMEM((nblk, pair), F32), pltpu.VMEM((2, nblk, blk), F32)],
        compiler_params=_cparams(("arbitrary", "arbitrary", "arbitrary")),
        name="moba_prompt",
    )(q, k, v)


_PG_CHUNK = 8


def _moba_sample_kernel(t_new, n_pages, page, pt_ref, q_ref, kn_ref, vn_ref, ck_hbm, cv_hbm, o_ref,
                        buf, sem, s_scr, p_scr, km_scr, bias_scr):
    b = pl.program_id(0)
    nb = pl.num_programs(0)
    n_chunks = n_pages // _PG_CHUNK
    n_tasks = 2 * n_chunks
    ppb = MOBA_BLOCK // page
    n_past = n_pages // ppb
    rows = t_new * A_HEADS

    def page_copy(src, bb, u, pg, slot):
        phys = pt_ref[bb, (u % n_chunks) * _PG_CHUNK + pg]
        return pltpu.make_async_copy(src.at[phys], buf.at[slot, pg], sem.at[slot])

    def start(bb, u, slot):
        src = ck_hbm if u < n_chunks else cv_hbm
        for pg in range(_PG_CHUNK):
            page_copy(src, bb, u, pg, slot).start()

    def wait(bb, u, slot):
        src = ck_hbm if u < n_chunks else cv_hbm
        for pg in range(_PG_CHUNK):
            page_copy(src, bb, u, pg, slot).wait()

    @pl.when(b == 0)
    def _():
        start(b, 0, 0)

    lane = lax.broadcasted_iota(jnp.int32, (A_HEADS, A_W), 1)
    sub = lax.broadcasted_iota(jnp.int32, (A_HEADS, A_W), 0)
    headmask = (lane // A_DIM == sub).astype(F32)
    q = q_ref[0]
    qbd = jnp.concatenate([jnp.broadcast_to(q[t:t + 1, :], (A_HEADS, A_W)) * headmask for t in range(t_new)],
                          axis=0)
    qbd_b = qbd.astype(BF16)

    acc = jnp.zeros((rows, A_W), F32)
    l = jnp.zeros((rows, 1), F32)
    p_own = None
    for u in range(n_tasks):
        slot = u % 2
        if u + 1 < n_tasks:
            start(b, u + 1, 1 - slot)
        else:
            @pl.when(b + 1 < nb)
            def _():
                start(b + 1, 0, 1 - slot)
        wait(b, u, slot)
        if u < n_chunks:
            def kpair(pr, c):
                ksum = jnp.zeros((1, A_W), F32)
                for e in range(ppb):
                    pg = pr * ppb + e
                    kp = buf[slot, pg]
                    s_scr[u * _PG_CHUNK + pg] = lax.dot_general(qbd_b, kp.astype(BF16), NT,
                                                               preferred_element_type=F32)
                    ksum = ksum + jnp.sum(kp, axis=0, keepdims=True)
                km_scr[pl.ds(u * (_PG_CHUNK // ppb) + pr, 1), :] = ksum * (1.0 / MOBA_BLOCK)
                return c
            lax.fori_loop(0, _PG_CHUNK // ppb, kpair, 0)
        if u == n_chunks - 1:
            g = lax.dot_general(qbd, km_scr[...], NT, precision=HI, preferred_element_type=F32)
            col = lax.broadcasted_iota(jnp.int32, (rows, n_past), 1)
            cnt = jnp.zeros((rows, n_past), jnp.int32)
            for jp in range(n_past):
                gj = g[:, jp:jp + 1]
                cnt = cnt + jnp.where((gj > g) | ((gj == g) & (jp < col)), 1, 0)
            selb = jnp.where(cnt < MOBA_TOPK, 0.0, NEG)
            for jb in range(n_past):
                bias_scr[jb] = jnp.broadcast_to(selb[:, jb:jb + 1], (rows, page))
            kn, vn = kn_ref[0], vn_ref[0]
            r_t = lax.broadcasted_iota(jnp.int32, (rows, 1), 0) // A_HEADS
            s_own = [jnp.where(r_t >= t, jnp.sum(qbd * kn[t:t + 1, :], axis=1, keepdims=True), NEG)
                     for t in range(t_new)]

            def smax(pg, mx):
                sb = s_scr[pg] + bias_scr[pg // ppb]
                s_scr[pg] = sb
                return jnp.maximum(mx, sb)
            mx = lax.fori_loop(0, n_pages, smax, jnp.full((rows, page), NEG, F32))
            m = jnp.max(mx, axis=1, keepdims=True)
            for t in range(t_new):
                m = jnp.maximum(m, s_own[t])

            def sexp(pg, ls):
                p = jnp.exp(s_scr[pg] - m)
                p_scr[pg] = p.astype(BF16)
                return ls + p
            ls = lax.fori_loop(0, n_pages, sexp, jnp.zeros((rows, page), F32))
            l = jnp.sum(ls, axis=1, keepdims=True)
            for t in range(t_new):
                p_t = jnp.exp(s_own[t] - m)
                l = l + p_t
                acc = acc + p_t * vn[t:t + 1, :]
        if u >= n_chunks:
            def vpage(pg, a):
                vp = buf[slot, pg].astype(BF16)
                return a + jnp.dot(p_scr[(u - n_chunks) * _PG_CHUNK + pg], vp, preferred_element_type=F32)
            acc = lax.fori_loop(0, _PG_CHUNK, vpage, acc)

    o = acc / l
    out_rows = [jnp.sum(o[t * A_HEADS:(t + 1) * A_HEADS, :] * headmask, axis=0, keepdims=True)
                for t in range(t_new)]
    out_rows.append(jnp.zeros((o_ref.shape[1] - t_new, A_W), F32))
    o_ref[0] = jnp.concatenate(out_rows, axis=0)


def _moba_sample(q8, kn8, vn8, ck, cv, page_table, t_new):
    db, tp, _ = q8.shape
    n_pages = page_table.shape[1]
    page = ck.shape[1]
    assert (n_pages * page) % MOBA_BLOCK == 0, "past length must end on a MoBA block boundary"
    assert MOBA_BLOCK % page == 0 and n_pages % _PG_CHUNK == 0 and _PG_CHUNK % (MOBA_BLOCK // page) == 0
    n_past = n_pages * page // MOBA_BLOCK
    assert n_past >= MOBA_TOPK
    rows = t_new * A_HEADS
    tok = pl.BlockSpec((1, tp, A_W), lambda b, pt: (b, 0, 0))
    any_spec = pl.BlockSpec(memory_space=pl.ANY)
    grid_spec = pltpu.PrefetchScalarGridSpec(
        num_scalar_prefetch=1,
        grid=(db,),
        in_specs=[tok, tok, tok, any_spec, any_spec],
        out_specs=tok,
        scratch_shapes=[pltpu.VMEM((2, _PG_CHUNK, page, A_W), F32), pltpu.SemaphoreType.DMA((2,)),
                        pltpu.VMEM((n_pages, rows, page), F32), pltpu.VMEM((n_pages, rows, page), BF16),
                        pltpu.VMEM((n_past, A_W), F32), pltpu.VMEM((n_past, rows, page), F32)],
    )
    return pl.pallas_call(
        functools.partial(_moba_sample_kernel, t_new, n_pages, page),
        grid_spec=grid_spec,
        out_shape=jax.ShapeDtypeStruct((db, tp, A_W), F32),
        compiler_params=_cparams(("arbitrary",)),
        name="moba_sample",
    )(page_table, q8, kn8, vn8, ck, cv)


def _gla_kernel(c, nch, carry, q_ref, k_ref, la_ref, v_ref, rg_ref, s0_ref, gg_ref, l_ref, e_ref, hm_ref,
                bd_ref, hv_ref, o_ref, sf_ref, st_scr, b_scr, qs_scr, k_scr, qb_scr, dch_scr, oo_scr):
    tt = c * nch
    j = pl.program_id(1)
    if carry:
        @pl.when(j == 0)
        def _():
            st_scr[...] = s0_ref[0]

    la = la_ref[0]
    b = jnp.dot(l_ref[...], la, precision=HI, preferred_element_type=F32)
    tot = jnp.dot(e_ref[...], la, precision=HI, preferred_element_type=F32)
    qs = q_ref[0] * (G_DK ** -0.5)
    k = k_ref[0]
    b_scr[...] = b
    qs_scr[...] = qs
    k_scr[...] = k
    qb_scr[...] = qs * jnp.exp(b)
    dch_scr[...] = jnp.exp(tot)
    kdec = (k * jnp.exp(tot - b)).astype(BF16)
    v_t = v_ref[0].T
    lane_chunk = lax.broadcasted_iota(jnp.int32, v_t.shape, 1) // c
    jrow = lax.broadcasted_iota(jnp.int32, (c, G_K), 0)

    def chunk(ci, st):
        r0 = pl.multiple_of(ci * c, c)
        if not carry:
            st = s0_ref[ci]
        o_inter = lax.dot_general(qb_scr[pl.ds(r0, c), :].astype(BF16), st.astype(BF16), NT,
                                  preferred_element_type=F32)
        b_c, k_c, qs_c = b_scr[pl.ds(r0, c), :], k_scr[pl.ds(r0, c), :], qs_scr[pl.ds(r0, c), :]
        terms = []
        for i in range(c):
            d = jnp.where(jrow <= i, b_c[i:i + 1, :] - b_c, NEG)
            terms.append(qs_c[i:i + 1, :] * k_c * jnp.exp(d))
        t_all = jnp.concatenate(terms, axis=0).astype(BF16)
        att = jnp.dot(t_all, hm_ref[...], preferred_element_type=F32)
        v_c = v_ref[0, pl.ds(r0, c), :]
        o_intra = jnp.sum(att.reshape(c, c, G_V) * v_c[None, :, :], axis=1)
        oo_scr[pl.ds(r0, c), :] = o_inter + o_intra
        v_m = jnp.where(lane_chunk == ci, v_t, 0.0).astype(BF16)
        upd = jnp.dot(v_m, kdec, preferred_element_type=F32) * bd_ref[...]
        st_new = st * dch_scr[pl.ds(r0, 1), :] + upd
        if not carry:
            sf_ref[ci] = st_new
        return st_new

    st = lax.fori_loop(0, nch, chunk, st_scr[...] if carry else jnp.zeros(st_scr.shape, F32))
    if carry:
        st_scr[...] = st

        @pl.when(j == pl.num_programs(1) - 1)
        def _():
            sf_ref[0] = st

    o = oo_scr[...]
    ms = jnp.dot(o * o, hv_ref[...], precision=HI, preferred_element_type=F32)
    rg = rg_ref[0]
    o_ref[0] = o * lax.rsqrt(ms + RMS_EPS) * gg_ref[...] * (rg / (1.0 + jnp.exp(-rg)))


def _gla(gq, gk, la, gv, rg, s0_t, gg, c, nch, carry):
    bz, t, _ = gq.shape
    tt = c * nch
    idx = jnp.arange(tt)
    same = (idx[:, None] // c) == (idx[None, :] // c)
    l_mat = (same & (idx[None, :] <= idx[:, None])).astype(F32)
    e_mat = same.astype(F32)
    hm = ((jnp.arange(G_K)[:, None] // G_DK) == (jnp.arange(G_V)[None, :] // G_DV)).astype(BF16)
    bd = ((jnp.arange(G_V)[:, None] // G_DV) == (jnp.arange(G_K)[None, :] // G_DK)).astype(F32)
    hv = ((jnp.arange(G_V)[:, None] // G_DV) == (jnp.arange(G_V)[None, :] // G_DV)).astype(F32) / G_DV
    tok = lambda w: pl.BlockSpec((1, tt, w), lambda b, j: (b, j, 0))
    full = lambda a: pl.BlockSpec(a.shape, lambda b, j: (0,) * a.ndim)
    ns = 1 if carry else nch
    st_spec = pl.BlockSpec((ns, G_V, G_K), lambda b, j: (b, 0, 0))
    return pl.pallas_call(
        functools.partial(_gla_kernel, c, nch, carry),
        grid=(bz, t // tt),
        in_specs=[tok(G_K), tok(G_K), tok(G_K), tok(G_V), tok(G_V), st_spec, full(gg), full(l_mat), full(e_mat),
                  full(hm), full(bd), full(hv)],
        out_specs=[tok(G_V), st_spec],
        out_shape=[jax.ShapeDtypeStruct((bz, t, G_V), F32), jax.ShapeDtypeStruct(s0_t.shape, F32)],
        scratch_shapes=[pltpu.VMEM((G_V, G_K), F32)] + [pltpu.VMEM((tt, G_K), F32)] * 5
                       + [pltpu.VMEM((tt, G_V), F32)],
        compiler_params=_cparams(("arbitrary", "arbitrary")),
        name="gla",
    )(gq, gk, la, gv, rg, s0_t, gg, l_mat, e_mat, hm, bd, hv)


def _state_to_t(s):
    n = s.shape[0]
    eye = jnp.eye(G_HEADS, dtype=s.dtype)
    return jnp.einsum("nhdv,hg->nhvgd", s, eye).reshape(n, G_V, G_K)


def _state_from_t(st):
    n = st.shape[0]
    s5 = st.reshape(n, G_HEADS, G_DV, G_HEADS, G_DK)
    return jnp.stack([s5[:, h, :, h, :] for h in range(G_HEADS)], axis=1).transpose(0, 1, 3, 2)


def _mem_attn_kernel(q_ref, mk_ref, mv_ref, o_ref):
    q = q_ref[0]
    mk = mk_ref[0].astype(BF16)
    mv = mv_ref[0].astype(BF16)
    lane = lax.broadcasted_iota(jnp.int32, q.shape, 1)
    out = jnp.zeros(q.shape, F32)
    for h in range(M_HEADS):
        hsel = (lane // M_DIM) == h
        s = lax.dot_general(jnp.where(hsel, q, 0.0).astype(BF16), mk, NT, preferred_element_type=F32)
        m = jnp.max(s, axis=1, keepdims=True)
        p = jnp.exp(s - m)
        l = jnp.sum(p, axis=1, keepdims=True)
        o_h = jnp.dot(p.astype(BF16), mv, preferred_element_type=F32)
        out = out + jnp.where(hsel, o_h / l, 0.0)
    o_ref[0] = out


def _mem_attn(qm, mk, mv, tm):
    bz, t, _ = qm.shape
    mem = mk.shape[1]
    return pl.pallas_call(
        _mem_attn_kernel,
        grid=(bz, t // tm),
        in_specs=[pl.BlockSpec((1, tm, M_W), lambda b, j: (b, j, 0)),
                  pl.BlockSpec((1, mem, M_W), lambda b, j: (b, 0, 0)),
                  pl.BlockSpec((1, mem, M_W), lambda b, j: (b, 0, 0))],
        out_specs=pl.BlockSpec((1, tm, M_W), lambda b, j: (b, j, 0)),
        out_shape=jax.ShapeDtypeStruct((bz, t, M_W), F32),
        compiler_params=_cparams(("arbitrary", "arbitrary")),
        name="mem_attn",
    )(qm, mk, mv)


def _mix_out_kernel(x_ref, oa_ref, og_ref, om_ref, lng_ref, lnb_ref, wo_ref, g1_ref, b1_ref, wr_ref, br_ref,
                    h1_ref, lg_ref):
    h = _layer_norm(x_ref[...], lng_ref[...], lnb_ref[...])
    o = jnp.dot(oa_ref[...].astype(BF16), wo_ref[0:A_W, :], preferred_element_type=F32)
    o = o + jnp.dot(og_ref[...].astype(BF16), wo_ref[A_W:A_W + G_V, :], preferred_element_type=F32)
    o = o + jnp.dot(om_ref[...].astype(BF16), wo_ref[A_W + G_V:, :], preferred_element_type=F32)
    h1 = _layer_norm(DEEPNORM_ALPHA * h + o, g1_ref[...], b1_ref[...])
    h1_ref[...] = h1
    lg_ref[...] = jnp.dot(h1, wr_ref[...], precision=HI, preferred_element_type=F32) + br_ref[...]


def _mix_out(x, oa, og, om, ln_g, ln_b, wo_bf16, g1, b1, wr_pad, br_pad, tm):
    n, d = x.shape
    tok = lambda w: pl.BlockSpec((tm, w), lambda i: (i, 0))
    full = lambda a: pl.BlockSpec(a.shape, lambda i: (0,) * a.ndim)
    return pl.pallas_call(
        _mix_out_kernel,
        grid=(n // tm,),
        in_specs=[tok(d), tok(A_W), tok(G_V), tok(M_W), full(ln_g), full(ln_b), full(wo_bf16), full(g1), full(b1),
                  full(wr_pad), full(br_pad)],
        out_specs=[tok(d), tok(LANES)],
        out_shape=[jax.ShapeDtypeStruct((n, d), F32), jax.ShapeDtypeStruct((n, LANES), F32)],
        compiler_params=_cparams(("arbitrary",)),
        name="mix_out",
    )(x, oa, og, om, ln_g, ln_b, wo_bf16, g1, b1, wr_pad, br_pad)


_FF_CHUNK = 512


def _moe_kernel(be_ref, nu_ref, x_ref, wg_ref, bg_ref, wu_ref, bu_ref, wd_ref, bd_ref, y_ref,
                wgb_scr, wub_scr, wdb_scr):
    i = pl.program_id(0)
    prev = be_ref[jnp.maximum(i - 1, 0)]

    @pl.when((i == 0) | (be_ref[i] != prev))
    def _():
        wgb_scr[...] = wg_ref[0].astype(BF16)
        wub_scr[...] = wu_ref[0].astype(BF16)
        wdb_scr[...] = wd_ref[0].astype(BF16)

    @pl.when(i < nu_ref[0])
    def _():
        x = x_ref[...]
        d_ff = wgb_scr.shape[1]
        y = jnp.zeros(y_ref.shape, F32) + bd_ref[0]
        for f0 in range(0, d_ff, _FF_CHUNK):
            f1 = f0 + _FF_CHUNK
            g = jnp.dot(x, wgb_scr[:, f0:f1], preferred_element_type=F32) + bg_ref[0, :, f0:f1]
            u = jnp.dot(x, wub_scr[:, f0:f1], preferred_element_type=F32) + bu_ref[0, :, f0:f1]
            g = jnp.minimum(g, SWIGLU_LIMIT)
            u = jnp.clip(u, -SWIGLU_LIMIT, SWIGLU_LIMIT)
            act = g * (1.0 / (1.0 + jnp.exp(-SWIGLU_ALPHA * g))) * (u + 1.0)
            y = y + jnp.dot(act.astype(BF16), wdb_scr[f0:f1, :], preferred_element_type=F32)
        y_ref[...] = y

    @pl.when(i >= nu_ref[0])
    def _():
        y_ref[...] = jnp.zeros(y_ref.shape, F32)


def _moe_experts(xs, block_e, n_used, wg, bg, wu, bu, wd, bd):
    rows, d = xs.shape
    tm = MOE_TM
    n_blocks = rows // tm
    d_ff = wg.shape[2]
    wspec = lambda a: pl.BlockSpec((1,) + a.shape[1:], lambda i, be, nu: (be[i], 0, 0))
    grid_spec = pltpu.PrefetchScalarGridSpec(
        num_scalar_prefetch=2,
        grid=(n_blocks,),
        in_specs=[pl.BlockSpec((tm, d), lambda i, be, nu: (i, 0)),
                  wspec(wg), wspec(bg), wspec(wu), wspec(bu), wspec(wd), wspec(bd)],
        out_specs=pl.BlockSpec((tm, d), lambda i, be, nu: (i, 0)),
        scratch_shapes=[pltpu.VMEM((d, d_ff), BF16), pltpu.VMEM((d, d_ff), BF16), pltpu.VMEM((d_ff, d), BF16)],
    )
    return pl.pallas_call(
        _moe_kernel,
        grid_spec=grid_spec,
        out_shape=jax.ShapeDtypeStruct((rows, d), F32),
        compiler_params=_cparams(("arbitrary",)),
        name="moe_experts",
    )(block_e, n_used, xs, wg, bg, wu, bu, wd, bd)


def _ln2_kernel(h_ref, f_ref, g_ref, b_ref, o_ref):
    o_ref[...] = _layer_norm(DEEPNORM_ALPHA * h_ref[...] + f_ref[...], g_ref[...], b_ref[...])


def _ln2(h1, f, g, b, tm):
    n, d = h1.shape
    tok = pl.BlockSpec((tm, d), lambda i: (i, 0))
    full = lambda a: pl.BlockSpec(a.shape, lambda i: (0,) * a.ndim)
    return pl.pallas_call(
        _ln2_kernel,
        grid=(n // tm,),
        in_specs=[tok, tok, full(g), full(b)],
        out_specs=tok,
        out_shape=jax.ShapeDtypeStruct((n, d), F32),
        compiler_params=_cparams(("arbitrary",)),
        name="ln2",
    )(h1, f, g, b)


def _route(logits):
    n = logits.shape[0]
    tm = MOE_TM
    top_val, top_idx = lax.top_k(logits, TOP_K)
    gate = jax.nn.softmax(top_val, axis=-1)
    flat_e = top_idx.reshape(-1)
    order = jnp.argsort(flat_e)
    sorted_e = flat_e[order]
    counts = jnp.bincount(flat_e, length=N_EXPERTS)
    start = jnp.cumsum(counts) - counts
    padded = (counts + tm - 1) // tm * tm
    pad_end = jnp.cumsum(padded)
    pad_start = pad_end - padded
    dest = (pad_start[sorted_e] + jnp.arange(n * TOP_K) - start[sorted_e]).astype(jnp.int32)
    n_blocks = -(-(n * TOP_K) // tm) + N_EXPERTS
    row_tok = jnp.full((n_blocks * tm,), n, jnp.int32).at[dest].set((order // TOP_K).astype(jnp.int32))
    block_e = jnp.minimum(jnp.searchsorted(pad_end, jnp.arange(n_blocks) * tm, side="right"),
                          N_EXPERTS - 1).astype(jnp.int32)
    n_used = (pad_end[-1] // tm).astype(jnp.int32).reshape(1)
    slot = jnp.zeros((n * TOP_K,), jnp.int32).at[order].set(dest)
    return gate, row_tok, block_e, n_used, slot


def kernel(x_prompt, x_sample, cache_k, cache_v, page_table, state_gla, cache_mem_k, cache_mem_v, mem_prompt,
           ln_in_g, ln_in_b, w_in, w_gla_gate, b_gla_gate, g_gla, w_mem_kv, w_out, ln1_g, ln1_b,
           w_router, b_router, w_gate, b_gate, w_up, b_up, w_down, b_down, ln2_g, ln2_b):
    assert w_in.shape[0] == DEPTH == 1
    bz, seq, d = x_prompt.shape
    db, t_new, _ = x_sample.shape
    n_phys, page = cache_k.shape[1], cache_k.shape[2]
    past = page_table.shape[1] * page
    mem_len = mem_prompt.shape[1]
    n_p, n_s = bz * seq, db * t_new
    t_pad = 8
    row = lambda a: a.reshape(1, -1)

    wi = w_in[0]
    c_lg = 3 * A_W + 2 * G_K + G_V
    w_perm = jnp.concatenate([wi[:, :c_lg], wi[:, c_lg + G_LOWRANK:], wi[:, c_lg:c_lg + G_LOWRANK],
                              jnp.zeros((d, LANES - G_LOWRANK), F32)], axis=1).astype(BF16)
    wgg = jnp.zeros((LANES, G_K), F32).at[:G_LOWRANK].set(w_gla_gate[0])
    bgg = row(b_gla_gate[0])
    gg = row(jnp.tile(g_gla[0], G_HEADS))
    wo = w_out[0].astype(BF16)
    wr = jnp.zeros((d, LANES), F32).at[:, :N_EXPERTS].set(w_router[0])
    br = jnp.full((1, LANES), NEG, F32).at[0, :N_EXPERTS].set(b_router[0])
    ln_g, ln_b = row(ln_in_g), row(ln_in_b)

    tabs_p = _rope_tables(jnp.arange(seq, dtype=jnp.int32))
    tabs_s = _rope_tables(jnp.tile(past + jnp.arange(t_new, dtype=jnp.int32), db))
    qp, kp, vp, gqp, gkp, gvp, lap, rgp, qmp = _project(x_prompt, tabs_p, ln_g, ln_b, w_perm, wgg, bgg, 512)
    xs3 = x_sample.reshape(1, n_s, d)
    qs, ks, vs, gqs, gks, gvs, las, rgs, qms = _project(xs3, tabs_s, ln_g, ln_b, w_perm, wgg, bgg, n_s)

    mkv = _matmul(mem_prompt.reshape(bz * mem_len, d), w_mem_kv[0].astype(BF16), 512)
    mk_p = mkv[:, :M_W].reshape(bz, mem_len, M_W)
    mv_p = mkv[:, M_W:].reshape(bz, mem_len, M_W)

    oa_p = _moba_prompt(qp, kp, vp)
    s0_p = jnp.zeros((bz, G_V, G_K), F32)
    og_p, st_p = _gla(gqp, gkp, lap, gvp, rgp, s0_p, gg, GLA_CHUNK, MOBA_BLOCK // GLA_CHUNK, True)
    om_p = _mem_attn(qmp, mk_p, mv_p, 512)

    pad_t = lambda a: jnp.pad(a.reshape(db, t_new, -1), ((0, 0), (0, t_pad - t_new), (0, 0)))
    oa_s = _moba_sample(pad_t(qs), pad_t(ks), pad_t(vs), cache_k[0].reshape(n_phys, page, A_W),
                        cache_v[0].reshape(n_phys, page, A_W), page_table, t_new)[:, :t_new]
    seq_per_step = 16
    grp = lambda a: pad_t(a).reshape(db // seq_per_step, seq_per_step * t_pad, -1)
    og_s, st_s = _gla(grp(gqs), grp(gks), grp(las), grp(gvs), grp(rgs), _state_to_t(state_gla[0]), gg,
                      t_pad, seq_per_step, False)
    og_s = og_s.reshape(db, t_pad, G_V)[:, :t_new]
    om_s = _mem_attn(pad_t(qms), cache_mem_k[0].reshape(db, mem_len, M_W),
                     cache_mem_v[0].reshape(db, mem_len, M_W), t_pad)[:, :t_new]

    cat = lambda a, b_: jnp.concatenate([a.reshape(n_p, -1), b_.reshape(n_s, -1)], axis=0)
    h1, logits = _mix_out(cat(x_prompt, x_sample), cat(oa_p, oa_s), cat(og_p, og_s), cat(om_p, om_s), ln_g, ln_b,
                          wo, row(ln1_g[0]), row(ln1_b[0]), wr, br, 512)

    n = n_p + n_s
    gate, row_tok, block_e, n_used, slot = _route(logits[:, :N_EXPERTS])
    xs_sorted = jnp.concatenate([h1.astype(BF16), jnp.zeros((1, d), BF16)], axis=0)[row_tok]
    b3 = lambda a: a[0].reshape(N_EXPERTS, 1, -1)
    yb = _moe_experts(xs_sorted, block_e, n_used, w_gate[0], b3(b_gate), w_up[0], b3(b_up), w_down[0], b3(b_down))
    f = jnp.einsum("nk,nkd->nd", gate, yb[slot].reshape(n, TOP_K, d))
    y = _ln2(h1, f, row(ln2_g[0]), row(ln2_b[0]), 512)

    return (y[:n_p].reshape(bz, seq, d), y[n_p:].reshape(db, t_new, d),
            kp.reshape(1, bz, seq, A_HEADS, A_DIM), vp.reshape(1, bz, seq, A_HEADS, A_DIM),
            _state_from_t(st_p)[None],
            mk_p.reshape(1, bz, mem_len, M_HEADS, M_DIM), mv_p.reshape(1, bz, mem_len, M_HEADS, M_DIM),
            ks.reshape(1, db, t_new, A_HEADS, A_DIM), vs.reshape(1, db, t_new, A_HEADS, A_DIM),
            _state_from_t(st_s)[None])
```

```python
import functools

import jax
import jax.numpy as jnp
from jax import lax
from jax.experimental import pallas as pl
from jax.experimental.pallas import tpu as pltpu

F32 = jnp.float32
BF16 = jnp.bfloat16
HI = lax.Precision.HIGHEST
NT = (((1,), (1,)), ((), ()))

A_HEADS, A_DIM = 8, 64
ROT_DIM = A_DIM // 4
ROPE_THETA = 500000.0
MOBA_BLOCK, MOBA_TOPK = 256, 3
G_HEADS, G_DK, G_DV, G_LOWRANK, G_TAU = 4, 32, 64, 16, 16.0
M_HEADS, M_DIM = 4, 64
A_W = A_HEADS * A_DIM
G_K = G_HEADS * G_DK
G_V = G_HEADS * G_DV
M_W = M_HEADS * M_DIM
N_EXPERTS, TOP_K = 32, 4
SWIGLU_ALPHA, SWIGLU_LIMIT = 1.702, 7.0
LN_EPS, RMS_EPS = 1e-5, 1e-6
NEG = -1e30
DEPTH = 1
DEEPNORM_ALPHA = (2 * DEPTH) ** 0.25

LANES = 128
VMEM_LIMIT = 56 * 1024 * 1024

TOK_TM = 512
GLA_CHUNK = 16
RT_GATE, RT_EID, RT_POS = 0, TOP_K, 2 * TOP_K


def _cparams(sem):
    return pltpu.CompilerParams(dimension_semantics=sem, vmem_limit_bytes=VMEM_LIMIT)


def _layer_norm(x, g, b):
    mu = jnp.mean(x, axis=-1, keepdims=True)
    xc = x - mu
    var = jnp.mean(xc * xc, axis=-1, keepdims=True)
    return xc * lax.rsqrt(var + LN_EPS) * g + b


_C_QA, _C_KA, _C_VA = 0, A_W, 2 * A_W
_C_QG = 3 * A_W
_C_KG = _C_QG + G_K
_C_VG = _C_KG + G_K
_C_RG = _C_VG + G_V
_C_QM = _C_RG + G_V
_C_LG = _C_QM + M_W
_C_END = _C_LG + LANES


def _proj_kernel(x_ref, g_ref, b_ref, w_ref, wgg_ref, bgg_ref, c_ref, s1_ref, s2_ref,
                 q_ref, kt_ref, vt_ref, gq_ref, gk_ref, gv_ref, la_ref, rg_ref, qm_ref):
    h = _layer_norm(x_ref[0], g_ref[...], b_ref[...]).astype(BF16)

    def mm(lo, hi):
        return jnp.dot(h, w_ref[:, lo:hi], preferred_element_type=F32)

    c, s1, s2 = c_ref[...], s1_ref[...], s2_ref[...]
    half = ROT_DIM // 2

    def rope(t):
        return t * c + pltpu.roll(t, A_W - half, 1) * s1 + pltpu.roll(t, half, 1) * s2

    q_ref[0] = rope(mm(_C_QA, _C_KA)) * (A_DIM ** -0.5)
    kt_ref[0] = rope(mm(_C_KA, _C_VA)).T
    vt_ref[0] = mm(_C_VA, _C_QG).T
    gq_ref[0] = mm(_C_QG, _C_KG)
    gk_ref[0] = mm(_C_KG, _C_VG)
    gv_ref[0] = mm(_C_VG, _C_RG)
    rg_ref[0] = mm(_C_RG, _C_QM)
    qm_ref[0] = mm(_C_QM, _C_LG) * (M_DIM ** -0.5)
    z = jnp.dot(mm(_C_LG, _C_END), wgg_ref[...], precision=HI, preferred_element_type=F32) + bgg_ref[...]
    la_ref[0] = (jnp.minimum(z, 0.0) - jnp.log(1.0 + jnp.exp(-jnp.abs(z)))) * (1.0 / G_TAU)


def _rope_tables(pos):
    half = ROT_DIM // 2
    inv = jnp.power(ROPE_THETA, -jnp.arange(half, dtype=F32) / half)
    ang = pos.astype(F32)[:, None] * inv[None, :]
    cos, sin = jnp.cos(ang), jnp.sin(ang)
    n = pos.shape[0]
    one = jnp.ones((n, A_DIM - ROT_DIM), F32)
    zero8 = jnp.zeros((n, half), F32)
    zero = jnp.zeros((n, A_DIM - ROT_DIM), F32)
    c = jnp.concatenate([cos, cos, one], axis=1)
    s1 = jnp.concatenate([-sin, zero8, zero], axis=1)
    s2 = jnp.concatenate([zero8, sin, zero], axis=1)
    return tuple(jnp.tile(t, (1, A_HEADS)) for t in (c, s1, s2))


def _project(x, pos_tables, ln_g, ln_b, w_perm, wgg, bgg, ts):
    bz, s, d = x.shape
    widths = (A_W, None, None, G_K, G_K, G_V, G_K, G_V, M_W)
    tok = lambda w: (pl.BlockSpec((1, ts, w), lambda j, b: (b, j, 0)) if w else
                     pl.BlockSpec((1, A_W, ts), lambda j, b: (b, 0, j)))
    full = lambda a: pl.BlockSpec(a.shape, lambda j, b: (0,) * a.ndim)
    tab = pl.BlockSpec((ts, A_W), lambda j, b: (j, 0))
    return pl.pallas_call(
        _proj_kernel,
        grid=(s // ts, bz),
        in_specs=[tok(d), full(ln_g), full(ln_b), full(w_perm), full(wgg), full(bgg), tab, tab, tab],
        out_specs=[tok(w) for w in widths],
        out_shape=[jax.ShapeDtypeStruct((bz, s, w) if w else (bz, A_W, s), F32) for w in widths],
        compiler_params=_cparams(("arbitrary", "arbitrary")),
        name="proj",
    )(x, ln_g, ln_b, w_perm, wgg, bgg, *pos_tables)


def _mem_kv_kernel(x_ref, w_ref, kt_ref, vt_ref):
    kv = jnp.dot(x_ref[0].astype(BF16), w_ref[...], preferred_element_type=F32)
    kt_ref[0] = kv[:, :M_W].T
    vt_ref[0] = kv[:, M_W:].T


def _mem_kv(mem, w_bf16):
    bz, m, d = mem.shape
    out = pl.BlockSpec((1, M_W, m), lambda b: (b, 0, 0))
    return pl.pallas_call(
        _mem_kv_kernel,
        grid=(bz,),
        in_specs=[pl.BlockSpec((1, m, d), lambda b: (b, 0, 0)), pl.BlockSpec(w_bf16.shape, lambda b: (0, 0))],
        out_specs=[out, out],
        out_shape=[jax.ShapeDtypeStruct((bz, M_W, m), F32)] * 2,
        compiler_params=_cparams(("arbitrary",)),
        name="mem_kv",
    )(mem, w_bf16)


_MOBA_HG = 4


def _moba_prompt_kernel(nblk, q_ref, kt_ref, vt_ref, o_ref, kb_scr, vt_scr, km_scr, selb_scr, acc_scr):
    blk = MOBA_BLOCK
    hg = _MOBA_HG
    i = pl.program_id(2)

    @pl.when(i == 0)
    def _():
        for j in range(nblk):
            kj = kt_ref[0, :, j * blk:(j + 1) * blk].T
            kb_scr[j] = kj.astype(BF16)
            km_scr[j:j + 1, :] = jnp.mean(kj, axis=0, keepdims=True)
            vt_scr[j] = vt_ref[0, :, j * blk:(j + 1) * blk].astype(BF16)

    q_t = q_ref[0].T
    row = lax.broadcasted_iota(jnp.int32, q_t.shape, 0)
    blk_iota = lax.broadcasted_iota(jnp.int32, (nblk, blk), 0)
    km = km_scr[...]
    q_tb = []
    for h in range(hg):
        q_h = jnp.where((row >= A_DIM * h) & (row < A_DIM * (h + 1)), q_t, 0.0)
        q_tb.append(q_h.astype(BF16))
        g = jnp.dot(km, q_h, precision=HI, preferred_element_type=F32)
        cnt = jnp.zeros((nblk, blk), jnp.int32)
        for jp in range(nblk):
            gj = g[jp:jp + 1, :]
            beats = (gj > g) | ((gj == g) & (jp < blk_iota))
            cnt = cnt + jnp.where(beats, 1, 0) * (jp < i).astype(jnp.int32)
        selb_scr[h] = jnp.where((blk_iota < i) & (cnt < MOBA_TOPK), 0.0, NEG)

    hs = lambda h: slice(A_DIM * h, A_DIM * (h + 1))
    kpos = lax.broadcasted_iota(jnp.int32, (blk, blk), 0)
    qpos = lax.broadcasted_iota(jnp.int32, (blk, blk), 1)
    k_own, vt_own = kb_scr[i], vt_scr[i]
    state = []
    for h in range(hg):
        s = jnp.dot(k_own, q_tb[h], preferred_element_type=F32)
        s = jnp.where(kpos <= qpos, s, NEG)
        m = jnp.max(s, axis=0, keepdims=True)
        p = jnp.exp(s - m)
        state += [m, jnp.sum(p, axis=0, keepdims=True)]
        acc_scr[h] = jnp.dot(vt_own[hs(h), :], p.astype(BF16), preferred_element_type=F32)

    def past(jj, carry):
        ka, kb_ = kb_scr[2 * jj], kb_scr[2 * jj + 1]
        vta, vtb = vt_scr[2 * jj], vt_scr[2 * jj + 1]
        new = []
        for h in range(hg):
            m, l = carry[2 * h:2 * h + 2]
            sa = jnp.dot(ka, q_tb[h], preferred_element_type=F32) + selb_scr[h, pl.ds(2 * jj, 1), :]
            sb = jnp.dot(kb_, q_tb[h], preferred_element_type=F32) + selb_scr[h, pl.ds(2 * jj + 1, 1), :]
            m_new = jnp.maximum(m, jnp.maximum(jnp.max(sa, axis=0, keepdims=True),
                                               jnp.max(sb, axis=0, keepdims=True)))
            alpha = jnp.exp(m - m_new)
            pa = jnp.exp(sa - m_new)
            pb = jnp.exp(sb - m_new)
            l = alpha * l + jnp.sum(pa, axis=0, keepdims=True) + jnp.sum(pb, axis=0, keepdims=True)
            acc_scr[h] = (alpha * acc_scr[h]
                          + jnp.dot(vta[hs(h), :], pa.astype(BF16), preferred_element_type=F32)
                          + jnp.dot(vtb[hs(h), :], pb.astype(BF16), preferred_element_type=F32))
            new += [m_new, l]
        return tuple(new)

    fin = lax.fori_loop(0, (i + 1) // 2, past, tuple(state))
    o_t = jnp.concatenate([acc_scr[h] / fin[2 * h + 1] for h in range(hg)], axis=0)
    o_ref[0] = o_t.T


def _moba_prompt(q, kt, vt):
    bz, s, _ = q.shape
    blk = MOBA_BLOCK
    nblk = s // blk
    assert nblk % 2 == 0
    w = _MOBA_HG * A_DIM
    return pl.pallas_call(
        functools.partial(_moba_prompt_kernel, nblk),
        grid=(bz, A_W // w, nblk),
        in_specs=[pl.BlockSpec((1, blk, w), lambda b, hp, i: (b, i, hp)),
                  pl.BlockSpec((1, w, s), lambda b, hp, i: (b, hp, 0)),
                  pl.BlockSpec((1, w, s), lambda b, hp, i: (b, hp, 0))],
        out_specs=pl.BlockSpec((1, blk, w), lambda b, hp, i: (b, i, hp)),
        out_shape=jax.ShapeDtypeStruct((bz, s, A_W), F32),
        scratch_shapes=[pltpu.VMEM((nblk, blk, w), BF16), pltpu.VMEM((nblk, w, blk), BF16),
                        pltpu.VMEM((nblk, w), F32), pltpu.VMEM((_MOBA_HG, nblk, blk), F32),
                        pltpu.VMEM((_MOBA_HG, A_DIM, blk), F32)],
        compiler_params=_cparams(("arbitrary", "arbitrary", "arbitrary")),
        name="moba_prompt",
    )(q, kt, vt)


_PG_CHUNK = 8


def _moba_sample_kernel(t_new, n_pages, page, pt_ref, q_ref, kn_ref, vn_ref, ck_hbm, cv_hbm, o_ref,
                        buf, sem, s_scr, p_scr, bias_scr):
    b = pl.program_id(0)
    nb = pl.num_programs(0)
    n_chunks = n_pages // _PG_CHUNK
    n_tasks = 2 * n_chunks
    ppb = MOBA_BLOCK // page
    n_past = n_pages // ppb
    rows = t_new * A_HEADS

    def page_copy(bb, u, pg, slot):
        src = ck_hbm if u < n_chunks else cv_hbm
        phys = pt_ref[bb, (u % n_chunks) * _PG_CHUNK + pg]
        return pltpu.make_async_copy(src.at[0, phys], buf.at[slot, pg], sem.at[slot])

    def start(bb, u, slot):
        for pg in range(_PG_CHUNK):
            page_copy(bb, u, pg, slot).start()

    def wait(bb, u, slot):
        for pg in range(_PG_CHUNK):
            page_copy(bb, u, pg, slot).wait()

    @pl.when(b == 0)
    def _():
        start(b, 0, 0)

    lane = lax.broadcasted_iota(jnp.int32, (A_HEADS, A_W), 1)
    sub = lax.broadcasted_iota(jnp.int32, (A_HEADS, A_W), 0)
    headmask = (lane // A_DIM == sub).astype(F32)
    q = q_ref[0]
    qbd = jnp.concatenate([jnp.broadcast_to(q[t:t + 1, :], (A_HEADS, A_W)) * headmask for t in range(t_new)],
                          axis=0)
    q_hi = qbd.astype(BF16)
    q_lo = (qbd - q_hi.astype(F32)).astype(BF16)
    q_hl = jnp.concatenate([q_hi, q_lo], axis=0)

    acc = jnp.zeros((rows, A_W), F32)
    l = jnp.zeros((rows, 1), F32)
    for u in range(n_tasks):
        slot = u % 2
        if u + 1 < n_tasks:
            start(b, u + 1, 1 - slot)
        else:
            @pl.when(b + 1 < nb)
            def _():
                start(b + 1, 0, 1 - slot)
        wait(b, u, slot)
        if u < n_chunks:
            for pg in range(_PG_CHUNK):
                s2 = jnp.dot(q_hl, buf[slot, pg].astype(BF16), preferred_element_type=F32)
                s_scr[u * _PG_CHUNK + pg] = s2[:rows] + s2[rows:]
        if u == n_chunks - 1:
            col = lax.broadcasted_iota(jnp.int32, (rows, n_past), 1)
            g = jnp.zeros((rows, n_past), F32)
            for jb in range(n_past):
                blk_s = s_scr[jb * ppb]
                for e in range(1, ppb):
                    blk_s = blk_s + s_scr[jb * ppb + e]
                g = jnp.where(col == jb, jnp.sum(blk_s, axis=1, keepdims=True), g)
            cnt = jnp.zeros((rows, n_past), jnp.int32)
            for jp in range(n_past):
                gj = g[:, jp:jp + 1]
                cnt = cnt + jnp.where((gj > g) | ((gj == g) & (jp < col)), 1, 0)
            selb = jnp.where(cnt < MOBA_TOPK, 0.0, NEG)
            for jb in range(n_past):
                bias_scr[jb] = jnp.broadcast_to(selb[:, jb:jb + 1], (rows, page))
            kn, vn = kn_ref[0], vn_ref[0]
            r_t = lax.broadcasted_iota(jnp.int32, (rows, 1), 0) // A_HEADS
            s_own = [jnp.where(r_t >= t, jnp.sum(qbd * kn[t:t + 1, :], axis=1, keepdims=True), NEG)
                     for t in range(t_new)]

            def smax(pg, mx):
                sb = s_scr[pg] + bias_scr[pg // ppb]
                s_scr[pg] = sb
                return jnp.maximum(mx, sb)
            mx = lax.fori_loop(0, n_pages, smax, jnp.full((rows, page), NEG, F32), unroll=4)
            m = jnp.max(mx, axis=1, keepdims=True)
            for t in range(t_new):
                m = jnp.maximum(m, s_own[t])

            def sexp(pg, ls):
                p = jnp.exp(s_scr[pg] - m)
                p_scr[pg] = p.astype(BF16)
                return ls + p
            ls = lax.fori_loop(0, n_pages, sexp, jnp.zeros((rows, page), F32), unroll=4)
            l = jnp.sum(ls, axis=1, keepdims=True)
            for t in range(t_new):
                p_t = jnp.exp(s_own[t] - m)
                l = l + p_t
                acc = acc + p_t * vn[t:t + 1, :]
        if u >= n_chunks:
            for pg in range(_PG_CHUNK):
                acc = acc + lax.dot_general(p_scr[(u - n_chunks) * _PG_CHUNK + pg], buf[slot, pg].astype(BF16), NT,
                                            preferred_element_type=F32)

    o = acc / l
    out_rows = [jnp.sum(o[t * A_HEADS:(t + 1) * A_HEADS, :] * headmask, axis=0, keepdims=True)
                for t in range(t_new)]
    out_rows.append(jnp.zeros((o_ref.shape[1] - t_new, A_W), F32))
    o_ref[0] = jnp.concatenate(out_rows, axis=0)


def _moba_sample(q8, kn8, vn8, ck, cv, page_table, t_new):
    db, tp, _ = q8.shape
    n_pages = page_table.shape[1]
    page = ck.shape[3]
    assert (n_pages * page) % MOBA_BLOCK == 0, "past length must end on a MoBA block boundary"
    assert MOBA_BLOCK % page == 0 and n_pages % _PG_CHUNK == 0
    n_past = n_pages * page // MOBA_BLOCK
    assert n_past >= MOBA_TOPK
    rows = t_new * A_HEADS
    tok = pl.BlockSpec((1, tp, A_W), lambda b, pt: (b, 0, 0))
    any_spec = pl.BlockSpec(memory_space=pl.ANY)
    grid_spec = pltpu.PrefetchScalarGridSpec(
        num_scalar_prefetch=1,
        grid=(db,),
        in_specs=[tok, tok, tok, any_spec, any_spec],
        out_specs=tok,
        scratch_shapes=[pltpu.VMEM((2, _PG_CHUNK, A_W, page), F32), pltpu.SemaphoreType.DMA((2,)),
                        pltpu.VMEM((n_pages, rows, page), F32), pltpu.VMEM((n_pages, rows, page), BF16),
                        pltpu.VMEM((n_past, rows, page), F32)],
    )
    return pl.pallas_call(
        functools.partial(_moba_sample_kernel, t_new, n_pages, page),
        grid_spec=grid_spec,
        out_shape=jax.ShapeDtypeStruct((db, tp, A_W), F32),
        compiler_params=_cparams(("arbitrary",)),
        name="moba_sample",
    )(page_table, q8, kn8, vn8, ck, cv)


def _gla_kernel(c, nch, carry, q_ref, k_ref, la_ref, v_ref, rg_ref, s0_ref, gg_ref, l_ref, e_ref, hm_ref,
                bd_ref, hv_ref, o_ref, sf_ref, st_scr, b_scr, qs_scr, k_scr, qb_scr, dch_scr, oo_scr):
    j = pl.program_id(1)
    if carry:
        @pl.when(j == 0)
        def _():
            st_scr[...] = s0_ref[0]

    la = la_ref[0]
    b = jnp.dot(l_ref[...], la, precision=HI, preferred_element_type=F32)
    tot = jnp.dot(e_ref[...], la, precision=HI, preferred_element_type=F32)
    qs = q_ref[0] * (G_DK ** -0.5)
    k = k_ref[0]
    b_scr[...] = b
    qs_scr[...] = qs
    k_scr[...] = k
    qb_scr[...] = qs * jnp.exp(b)
    dch_scr[...] = jnp.exp(tot)
    kdec = (k * jnp.exp(tot - b)).astype(BF16)
    v_t = v_ref[0].T
    lane_chunk = lax.broadcasted_iota(jnp.int32, v_t.shape, 1) // c
    jrow = lax.broadcasted_iota(jnp.int32, (c, G_K), 0)

    def chunk(ci, st):
        r0 = pl.multiple_of(ci * c, c)
        if not carry:
            st = s0_ref[ci]
        o_inter = lax.dot_general(qb_scr[pl.ds(r0, c), :].astype(BF16), st.astype(BF16), NT,
                                  preferred_element_type=F32)
        b_c, k_c, qs_c = b_scr[pl.ds(r0, c), :], k_scr[pl.ds(r0, c), :], qs_scr[pl.ds(r0, c), :]
        terms = []
        for i in range(c):
            d = jnp.where(jrow <= i, b_c[i:i + 1, :] - b_c, NEG)
            terms.append(qs_c[i:i + 1, :] * k_c * jnp.exp(d))
        t_all = jnp.concatenate(terms, axis=0).astype(BF16)
        att = jnp.dot(t_all, hm_ref[...], preferred_element_type=F32)
        v_c = v_ref[0, pl.ds(r0, c), :]
        o_intra = jnp.sum(att.reshape(c, c, G_V) * v_c[None, :, :], axis=1)
        oo_scr[pl.ds(r0, c), :] = o_inter + o_intra
        v_m = jnp.where(lane_chunk == ci, v_t, 0.0).astype(BF16)
        upd = jnp.dot(v_m, kdec, preferred_element_type=F32) * bd_ref[...]
        st_new = st * dch_scr[pl.ds(r0, 1), :] + upd
        if not carry:
            sf_ref[ci] = st_new
        return st_new

    st = lax.fori_loop(0, nch, chunk, st_scr[...] if carry else jnp.zeros(st_scr.shape, F32))
    if carry:
        st_scr[...] = st

        @pl.when(j == pl.num_programs(1) - 1)
        def _():
            sf_ref[0] = st

    o = oo_scr[...]
    ms = jnp.dot(o * o, hv_ref[...], precision=HI, preferred_element_type=F32)
    rg = rg_ref[0]
    o_ref[0] = o * lax.rsqrt(ms + RMS_EPS) * gg_ref[...] * (rg / (1.0 + jnp.exp(-rg)))


def _gla(gq, gk, la, gv, rg, s0_t, gg, c, nch, carry):
    bz, t, _ = gq.shape
    tt = c * nch
    idx = jnp.arange(tt)
    same = (idx[:, None] // c) == (idx[None, :] // c)
    l_mat = (same & (idx[None, :] <= idx[:, None])).astype(F32)
    e_mat = same.astype(F32)
    hm = ((jnp.arange(G_K)[:, None] // G_DK) == (jnp.arange(G_V)[None, :] // G_DV)).astype(BF16)
    bd = ((jnp.arange(G_V)[:, None] // G_DV) == (jnp.arange(G_K)[None, :] // G_DK)).astype(F32)
    hv = ((jnp.arange(G_V)[:, None] // G_DV) == (jnp.arange(G_V)[None, :] // G_DV)).astype(F32) / G_DV
    tok = lambda w: pl.BlockSpec((1, tt, w), lambda b, j: (b, j, 0))
    full = lambda a: pl.BlockSpec(a.shape, lambda b, j: (0,) * a.ndim)
    ns = 1 if carry else nch
    st_spec = pl.BlockSpec((ns, G_V, G_K), lambda b, j: (b, 0, 0))
    return pl.pallas_call(
        functools.partial(_gla_kernel, c, nch, carry),
        grid=(bz, t // tt),
        in_specs=[tok(G_K), tok(G_K), tok(G_K), tok(G_V), tok(G_V), st_spec, full(gg), full(l_mat), full(e_mat),
                  full(hm), full(bd), full(hv)],
        out_specs=[tok(G_V), st_spec],
        out_shape=[jax.ShapeDtypeStruct((bz, t, G_V), F32), jax.ShapeDtypeStruct(s0_t.shape, F32)],
        scratch_shapes=[pltpu.VMEM((G_V, G_K), F32)] + [pltpu.VMEM((tt, G_K), F32)] * 5
                       + [pltpu.VMEM((tt, G_V), F32)],
        compiler_params=_cparams(("arbitrary", "arbitrary")),
        name="gla",
    )(gq, gk, la, gv, rg, s0_t, gg, l_mat, e_mat, hm, bd, hv)


def _state_to_t(s):
    n = s.shape[0]
    eye = jnp.eye(G_HEADS, dtype=s.dtype)
    return jnp.einsum("nhdv,hg->nhvgd", s, eye).reshape(n, G_V, G_K)


def _state_from_t(st):
    n = st.shape[0]
    s5 = st.reshape(n, G_HEADS, G_DV, G_HEADS, G_DK)
    return jnp.stack([s5[:, h, :, h, :] for h in range(G_HEADS)], axis=1).transpose(0, 1, 3, 2)


def _mem_attn_kernel(q_ref, mkt_ref, mvt_ref, o_ref):
    q = q_ref[0]
    mkt = mkt_ref[0].astype(BF16)
    mvt = mvt_ref[0].astype(BF16)
    lane = lax.broadcasted_iota(jnp.int32, q.shape, 1)
    out = jnp.zeros(q.shape, F32)
    for h in range(M_HEADS):
        hsel = (lane // M_DIM) == h
        s = jnp.dot(jnp.where(hsel, q, 0.0).astype(BF16), mkt, preferred_element_type=F32)
        m = jnp.max(s, axis=1, keepdims=True)
        p = jnp.exp(s - m)
        l = jnp.sum(p, axis=1, keepdims=True)
        o_h = lax.dot_general(p.astype(BF16), mvt, NT, preferred_element_type=F32)
        out = out + jnp.where(hsel, o_h / l, 0.0)
    o_ref[0] = out


def _mem_attn(qm, mkt, mvt, tm):
    bz, t, _ = qm.shape
    mem = mkt.shape[2]
    return pl.pallas_call(
        _mem_attn_kernel,
        grid=(bz, t // tm),
        in_specs=[pl.BlockSpec((1, tm, M_W), lambda b, j: (b, j, 0)),
                  pl.BlockSpec((1, M_W, mem), lambda b, j: (b, 0, 0)),
                  pl.BlockSpec((1, M_W, mem), lambda b, j: (b, 0, 0))],
        out_specs=pl.BlockSpec((1, tm, M_W), lambda b, j: (b, j, 0)),
        out_shape=jax.ShapeDtypeStruct((bz, t, M_W), F32),
        compiler_params=_cparams(("arbitrary", "arbitrary")),
        name="mem_attn",
    )(qm, mkt, mvt)


def _mix_out_kernel(x_ref, oa_ref, og_ref, om_ref, lng_ref, lnb_ref, wo_ref, g1_ref, b1_ref, wr_ref, br_ref,
                    tri_ref, h1_ref, rt_ref, cnt_ref, carry_scr):
    @pl.when(pl.program_id(0) == 0)
    def _():
        carry_scr[...] = jnp.zeros(carry_scr.shape, F32)

    h = _layer_norm(x_ref[...], lng_ref[...], lnb_ref[...])
    o = jnp.dot(oa_ref[...].astype(BF16), wo_ref[0:A_W, :], preferred_element_type=F32)
    o = o + jnp.dot(og_ref[...].astype(BF16), wo_ref[A_W:A_W + G_V, :], preferred_element_type=F32)
    o = o + jnp.dot(om_ref[...].astype(BF16), wo_ref[A_W + G_V:, :], preferred_element_type=F32)
    h1 = _layer_norm(DEEPNORM_ALPHA * h + o, g1_ref[...], b1_ref[...])
    h1_ref[...] = h1
    work = jnp.dot(h1, wr_ref[...], precision=HI, preferred_element_type=F32) + br_ref[...]
    lane = lax.broadcasted_iota(jnp.int32, work.shape, 1).astype(F32)
    onehot = jnp.zeros(work.shape, F32)
    vals, eids = [], []
    for _ in range(TOP_K):
        mk = jnp.max(work, axis=1, keepdims=True)
        ek = jnp.min(jnp.where(work == mk, lane, float(LANES)), axis=1, keepdims=True)
        hit = lane == ek
        onehot = jnp.where(hit, 1.0, onehot)
        work = jnp.where(hit, -jnp.inf, work)
        vals.append(mk)
        eids.append(ek)
    ex = [jnp.exp(v - vals[0]) for v in vals]
    den = ex[0] + ex[1] + ex[2] + ex[3]
    pos_full = jnp.dot(tri_ref[...], onehot.astype(BF16), preferred_element_type=F32) + carry_scr[...]
    carry_scr[...] = carry_scr[...] + jnp.sum(onehot, axis=0, keepdims=True)
    cnt_ref[...] = carry_scr[...]
    rt = jnp.zeros(work.shape, F32)
    for k in range(TOP_K):
        pk = jnp.sum(jnp.where(lane == eids[k], pos_full, 0.0), axis=1, keepdims=True)
        rt = jnp.where(lane == RT_GATE + k, ex[k] / den, rt)
        rt = jnp.where(lane == RT_EID + k, eids[k], rt)
        rt = jnp.where(lane == RT_POS + k, pk, rt)
    rt_ref[...] = rt


def _mix_out(x, oa, og, om, ln_g, ln_b, wo_bf16, g1, b1, wr_pad, br_pad, tm):
    n, d = x.shape
    tri = (jnp.arange(tm)[None, :] < jnp.arange(tm)[:, None]).astype(BF16)
    tok = lambda w: pl.BlockSpec((tm, w), lambda i: (i, 0))
    full = lambda a: pl.BlockSpec(a.shape, lambda i: (0,) * a.ndim)
    return pl.pallas_call(
        _mix_out_kernel,
        grid=(n // tm,),
        in_specs=[tok(d), tok(A_W), tok(G_V), tok(M_W), full(ln_g), full(ln_b), full(wo_bf16), full(g1), full(b1),
                  full(wr_pad), full(br_pad), full(tri)],
        out_specs=[tok(d), tok(LANES), pl.BlockSpec((1, LANES), lambda i: (0, 0))],
        out_shape=[jax.ShapeDtypeStruct((n, d), F32), jax.ShapeDtypeStruct((n, LANES), F32),
                   jax.ShapeDtypeStruct((1, LANES), F32)],
        scratch_shapes=[pltpu.VMEM((1, LANES), F32)],
        compiler_params=_cparams(("arbitrary",)),
        name="mix_out",
    )(x, oa, og, om, ln_g, ln_b, wo_bf16, g1, b1, wr_pad, br_pad, tri)


def _row_copy(src, s_row, dst, d_row, sem):
    return pltpu.make_async_copy(src.at[pl.ds(s_row, 1)], dst.at[pl.ds(d_row, 1)], sem)


def _dispatch_kernel(zs_ref, ze_ref, dest_ref, h1_ref, xs_hbm, zero_scr, sems):
    tm = h1_ref.shape[0]
    sem, zsem = sems.at[0], sems.at[1]

    def issue(r, c):
        for k in range(TOP_K):
            _row_copy(h1_ref, r, xs_hbm, dest_ref[r * TOP_K + k], sem).start()
        return c
    lax.fori_loop(0, tm, issue, 0, unroll=4)

    @pl.when(pl.program_id(0) == pl.num_programs(0) - 1)
    def _():
        zero_scr[...] = jnp.zeros(zero_scr.shape, F32)

        def zero_rows(wait):
            def per_row(r, c):
                cp = _row_copy(zero_scr, 0, xs_hbm, r, zsem)
                if wait:
                    cp.wait()
                else:
                    cp.start()
                return c

            def per_expert(e, c):
                return lax.fori_loop(zs_ref[e], ze_ref[e], per_row, c)
            lax.fori_loop(0, N_EXPERTS, per_expert, 0)
        zero_rows(False)
        zero_rows(True)

    def drain(r, c):
        for k in range(TOP_K):
            _row_copy(h1_ref, r, xs_hbm, 0, sem).wait()
        return c
    lax.fori_loop(0, tm, drain, 0, unroll=4)


def _dispatch(h1, dest_flat, z_start, z_end, n_rows, tm):
    n, d = h1.shape
    grid_spec = pltpu.PrefetchScalarGridSpec(
        num_scalar_prefetch=2,
        grid=(n // tm,),
        in_specs=[pl.BlockSpec((tm * TOP_K,), lambda i, zs, ze: (i,), memory_space=pltpu.SMEM),
                  pl.BlockSpec((tm, d), lambda i, zs, ze: (i, 0))],
        out_specs=pl.BlockSpec(memory_space=pl.ANY),
        scratch_shapes=[pltpu.VMEM((8, d), F32), pltpu.SemaphoreType.DMA((2,))],
    )
    return pl.pallas_call(
        _dispatch_kernel,
        grid_spec=grid_spec,
        out_shape=jax.ShapeDtypeStruct((n_rows, d), F32),
        compiler_params=_cparams(("arbitrary",)),
        name="moe_dispatch",
    )(z_start, z_end, dest_flat, h1)


_FF_CHUNK = 512


def _moe_kernel(be_ref, nu_ref, x_ref, wg_ref, bg_ref, wu_ref, bu_ref, wd_ref, bd_ref, y_ref,
                wgb_scr, wub_scr, wdb_scr):
    i = pl.program_id(0)
    prev = be_ref[jnp.maximum(i - 1, 0)]

    @pl.when((i == 0) | (be_ref[i] != prev))
    def _():
        wgb_scr[...] = wg_ref[0].astype(BF16)
        wub_scr[...] = wu_ref[0].astype(BF16)
        wdb_scr[...] = wd_ref[0].astype(BF16)

    @pl.when(i < nu_ref[0])
    def _():
        x = x_ref[...].astype(BF16)
        d_ff = wgb_scr.shape[1]
        y = jnp.zeros(y_ref.shape, F32) + bd_ref[0]
        for f0 in range(0, d_ff, _FF_CHUNK):
            f1 = f0 + _FF_CHUNK
            g = jnp.dot(x, wgb_scr[:, f0:f1], preferred_element_type=F32) + bg_ref[0, :, f0:f1]
            u = jnp.dot(x, wub_scr[:, f0:f1], preferred_element_type=F32) + bu_ref[0, :, f0:f1]
            g = jnp.minimum(g, SWIGLU_LIMIT)
            u = jnp.clip(u, -SWIGLU_LIMIT, SWIGLU_LIMIT)
            act = g * (1.0 / (1.0 + jnp.exp(-SWIGLU_ALPHA * g))) * (u + 1.0)
            y = y + jnp.dot(act.astype(BF16), wdb_scr[f0:f1, :], preferred_element_type=F32)
        y_ref[...] = y

    @pl.when(i >= nu_ref[0])
    def _():
        y_ref[...] = jnp.zeros(y_ref.shape, F32)


def _moe_experts(xs, block_e, n_used, wg, bg, wu, bu, wd, bd, tm):
    rows, d = xs.shape
    n_blocks = rows // tm
    d_ff = wg.shape[2]
    wspec = lambda a: pl.BlockSpec((1,) + a.shape[1:], lambda i, be, nu: (be[i], 0, 0))
    grid_spec = pltpu.PrefetchScalarGridSpec(
        num_scalar_prefetch=2,
        grid=(n_blocks,),
        in_specs=[pl.BlockSpec((tm, d), lambda i, be, nu: (jnp.minimum(i, nu[0] - 1), 0)),
                  wspec(wg), wspec(bg), wspec(wu), wspec(bu), wspec(wd), wspec(bd)],
        out_specs=pl.BlockSpec((tm, d), lambda i, be, nu: (i, 0)),
        scratch_shapes=[pltpu.VMEM((d, d_ff), BF16), pltpu.VMEM((d, d_ff), BF16), pltpu.VMEM((d_ff, d), BF16)],
    )
    return pl.pallas_call(
        _moe_kernel,
        grid_spec=grid_spec,
        out_shape=jax.ShapeDtypeStruct((rows, d), F32),
        compiler_params=_cparams(("arbitrary",)),
        name="moe_experts",
    )(block_e, n_used, xs, wg, bg, wu, bu, wd, bd)


def _combine_kernel(dcur_ref, dnxt_ref, h1_ref, rt_ref, g_ref, b_ref, yb_hbm, o_ref, gbuf, sem):
    i = pl.program_id(0)
    tm = h1_ref.shape[0]

    def gather(dref, slot, wait):
        def body(r, c):
            for k in range(TOP_K):
                cp = pltpu.make_async_copy(yb_hbm.at[pl.ds(dref[r * TOP_K + k], 1)],
                                           gbuf.at[slot, k, pl.ds(r, 1)], sem.at[slot])
                if wait:
                    cp.wait()
                else:
                    cp.start()
            return c
        lax.fori_loop(0, tm, body, 0, unroll=4)

    @pl.when(i == 0)
    def _():
        gather(dcur_ref, 0, False)

    @pl.when(i + 1 < pl.num_programs(0))
    def _():
        gather(dnxt_ref, (i + 1) % 2, False)

    slot = i % 2
    gather(dcur_ref, slot, True)
    rt = rt_ref[...]
    f = rt[:, RT_GATE:RT_GATE + 1] * gbuf[slot, 0]
    for k in range(1, TOP_K):
        f = f + rt[:, RT_GATE + k:RT_GATE + k + 1] * gbuf[slot, k]
    o_ref[...] = _layer_norm(DEEPNORM_ALPHA * h1_ref[...] + f, g_ref[...], b_ref[...])


def _combine_ln2(h1, rt, dest_flat, yb, g, b, tm):
    n, d = h1.shape
    nt = n // tm
    tok = lambda w: pl.BlockSpec((tm, w), lambda i: (i, 0))
    full = lambda a: pl.BlockSpec(a.shape, lambda i: (0,) * a.ndim)
    return pl.pallas_call(
        _combine_kernel,
        grid=(nt,),
        in_specs=[pl.BlockSpec((tm * TOP_K,), lambda i: (i,), memory_space=pltpu.SMEM),
                  pl.BlockSpec((tm * TOP_K,), lambda i: (jnp.minimum(i + 1, nt - 1),), memory_space=pltpu.SMEM),
                  tok(d), tok(LANES), full(g), full(b), pl.BlockSpec(memory_space=pl.ANY)],
        out_specs=tok(d),
        out_shape=jax.ShapeDtypeStruct((n, d), F32),
        scratch_shapes=[pltpu.VMEM((2, TOP_K, tm, d), F32), pltpu.SemaphoreType.DMA((2,))],
        compiler_params=_cparams(("arbitrary",)),
        name="moe_combine_ln2",
    )(dest_flat, dest_flat, h1, rt, g, b, yb)


def _layout(rt, counts, n, tm):
    counts = counts.astype(jnp.int32)
    padded = (counts + tm - 1) // tm * tm
    pad_end = jnp.cumsum(padded)
    pad_start = pad_end - padded
    eid = rt[:, RT_EID:RT_EID + TOP_K].astype(jnp.int32)
    pos = rt[:, RT_POS:RT_POS + TOP_K].astype(jnp.int32)
    dest = (pad_start[eid] + pos).reshape(-1)
    n_blocks = -(-(n * TOP_K) // tm) + N_EXPERTS
    block_e = jnp.minimum(jnp.searchsorted(pad_end, jnp.arange(n_blocks) * tm, side="right"),
                          N_EXPERTS - 1).astype(jnp.int32)
    n_used = (pad_end[-1] // tm).astype(jnp.int32).reshape(1)
    return dest, block_e, n_used, (pad_start + counts).astype(jnp.int32), pad_end.astype(jnp.int32), n_blocks * tm


def _hd_rows(a):
    nd = a.ndim
    t = jnp.transpose(a, tuple(range(nd - 3)) + (nd - 2, nd - 1, nd - 3))
    return t.reshape(t.shape[:-3] + (t.shape[-3] * t.shape[-2], t.shape[-1]))


def _rows_hd(at, heads):
    t = at.reshape(at.shape[:-2] + (heads, at.shape[-2] // heads, at.shape[-1]))
    nd = t.ndim
    return jnp.transpose(t, tuple(range(nd - 3)) + (nd - 1, nd - 3, nd - 2))


def kernel(x_prompt, x_sample, cache_k, cache_v, page_table, state_gla, cache_mem_k, cache_mem_v, mem_prompt,
           ln_in_g, ln_in_b, w_in, w_gla_gate, b_gla_gate, g_gla, w_mem_kv, w_out, ln1_g, ln1_b,
           w_router, b_router, w_gate, b_gate, w_up, b_up, w_down, b_down, ln2_g, ln2_b):
    assert w_in.shape[0] == DEPTH == 1
    bz, seq, d = x_prompt.shape
    db, t_new, _ = x_sample.shape
    page = cache_k.shape[2]
    past = page_table.shape[1] * page
    mem_len = mem_prompt.shape[1]
    n_p, n_s = bz * seq, db * t_new
    n = n_p + n_s
    tm = TOK_TM
    t_pad = 8
    row = lambda a: a.reshape(1, -1)

    wi = w_in[0]
    c_lg = 3 * A_W + 2 * G_K + G_V
    w_perm = jnp.concatenate([wi[:, :c_lg], wi[:, c_lg + G_LOWRANK:], wi[:, c_lg:c_lg + G_LOWRANK],
                              jnp.zeros((d, LANES - G_LOWRANK), F32)], axis=1).astype(BF16)
    wgg = jnp.zeros((LANES, G_K), F32).at[:G_LOWRANK].set(w_gla_gate[0])
    bgg = row(b_gla_gate[0])
    gg = row(jnp.tile(g_gla[0], G_HEADS))
    wo = w_out[0].astype(BF16)
    wr = jnp.zeros((d, LANES), F32).at[:, :N_EXPERTS].set(w_router[0])
    br = jnp.full((1, LANES), -jnp.inf, F32).at[0, :N_EXPERTS].set(b_router[0])
    ln_g, ln_b = row(ln_in_g), row(ln_in_b)

    tabs_p = _rope_tables(jnp.arange(seq, dtype=jnp.int32))
    tabs_s = _rope_tables(jnp.tile(past + jnp.arange(t_new, dtype=jnp.int32), db))
    qp, ktp, vtp, gqp, gkp, gvp, lap, rgp, qmp = _project(x_prompt, tabs_p, ln_g, ln_b, w_perm, wgg, bgg, tm)
    xs3 = x_sample.reshape(1, n_s, d)
    qs, kts, vts, gqs, gks, gvs, las, rgs, qms = _project(xs3, tabs_s, ln_g, ln_b, w_perm, wgg, bgg, n_s)
    ks, vs = kts[0].T, vts[0].T
    mkt_p, mvt_p = _mem_kv(mem_prompt, w_mem_kv[0].astype(BF16))

    oa_p = _moba_prompt(qp, ktp, vtp)
    s0_p = jnp.zeros((bz, G_V, G_K), F32)
    og_p, st_p = _gla(gqp, gkp, lap, gvp, rgp, s0_p, gg, GLA_CHUNK, MOBA_BLOCK // GLA_CHUNK, True)
    om_p = _mem_attn(qmp, mkt_p, mvt_p, tm)

    pad_t = lambda a: jnp.pad(a.reshape(db, t_new, -1), ((0, 0), (0, t_pad - t_new), (0, 0)))
    oa_s = _moba_sample(pad_t(qs), pad_t(ks), pad_t(vs), _hd_rows(cache_k), _hd_rows(cache_v), page_table,
                        t_new)[:, :t_new]
    seq_per_step = 16
    grp = lambda a: pad_t(a).reshape(db // seq_per_step, seq_per_step * t_pad, -1)
    og_s, st_s = _gla(grp(gqs), grp(gks), grp(las), grp(gvs), grp(rgs), _state_to_t(state_gla[0]), gg,
                      t_pad, seq_per_step, False)
    og_s = og_s.reshape(db, t_pad, G_V)[:, :t_new]
    om_s = _mem_attn(pad_t(qms), _hd_rows(cache_mem_k[0]), _hd_rows(cache_mem_v[0]), t_pad)[:, :t_new]

    cat = lambda a, b_: jnp.concatenate([a.reshape(n_p, -1), b_.reshape(n_s, -1)], axis=0)
    h1, rt, cnt = _mix_out(cat(x_prompt, x_sample), cat(oa_p, oa_s), cat(og_p, og_s), cat(om_p, om_s), ln_g, ln_b,
                           wo, row(ln1_g[0]), row(ln1_b[0]), wr, br, tm)

    dest, block_e, n_used, z_start, z_end, n_rows = _layout(rt, cnt[0, :N_EXPERTS], n, tm)
    xs_sorted = _dispatch(h1, dest, z_start, z_end, n_rows, tm)
    b3 = lambda a: a[0].reshape(N_EXPERTS, 1, -1)
    yb = _moe_experts(xs_sorted, block_e, n_used, w_gate[0], b3(b_gate), w_up[0], b3(b_up), w_down[0], b3(b_down), tm)
    y = _combine_ln2(h1, rt, dest, yb, row(ln2_g[0]), row(ln2_b[0]), tm)

    return (y[:n_p].reshape(bz, seq, d), y[n_p:].reshape(db, t_new, d),
            _rows_hd(ktp, A_HEADS)[None], _rows_hd(vtp, A_HEADS)[None],
            _state_from_t(st_p)[None],
            _rows_hd(mkt_p, M_HEADS)[None], _rows_hd(mvt_p, M_HEADS)[None],
            ks.reshape(1, db, t_new, A_HEADS, A_DIM), vs.reshape(1, db, t_new, A_HEADS, A_DIM),
            _state_from_t(st_s)[None])
```

```python
import functools

import jax
import jax.numpy as jnp
import numpy as np
from jax import lax
from jax.experimental import pallas as pl
from jax.experimental.pallas import tpu as pltpu

F32 = jnp.float32
BF16 = jnp.bfloat16
HI = lax.Precision.HIGHEST
NT = (((1,), (1,)), ((), ()))

A_HEADS, A_DIM = 8, 64
ROT_DIM = A_DIM // 4
ROPE_THETA = 500000.0
MOBA_BLOCK, MOBA_TOPK = 256, 3
G_HEADS, G_DK, G_DV, G_LOWRANK, G_TAU = 4, 32, 64, 16, 16.0
M_HEADS, M_DIM = 4, 64
A_W = A_HEADS * A_DIM
G_K = G_HEADS * G_DK
G_V = G_HEADS * G_DV
M_W = M_HEADS * M_DIM
N_EXPERTS, TOP_K = 32, 4
SWIGLU_ALPHA, SWIGLU_LIMIT = 1.702, 7.0
LN_EPS, RMS_EPS = 1e-5, 1e-6
NEG = -1e30
LOG2E = 1.4426950408889634
DEPTH = 1
DEEPNORM_ALPHA = (2 * DEPTH) ** 0.25

LANES = 128
VMEM_LIMIT = 56 * 1024 * 1024

TOK_TM = 512
GLA_CHUNK = 16
RT_GATE, RT_EID, RT_POS = 0, TOP_K, 2 * TOP_K


def _cparams(sem):
    return pltpu.CompilerParams(dimension_semantics=sem, vmem_limit_bytes=VMEM_LIMIT)


def _layer_norm(x, g, b):
    mu = jnp.mean(x, axis=-1, keepdims=True)
    xc = x - mu
    var = jnp.mean(xc * xc, axis=-1, keepdims=True)
    return xc * lax.rsqrt(var + LN_EPS) * g + b


_C_QA, _C_KA, _C_VA = 0, A_W, 2 * A_W
_C_QG = 3 * A_W
_C_KG = _C_QG + G_K
_C_VG = _C_KG + G_K
_C_RG = _C_VG + G_V
_C_QM = _C_RG + G_V
_C_LG = _C_QM + M_W
_C_END = _C_LG + LANES


def _proj_kernel(x_ref, g_ref, b_ref, w_ref, wgg_ref, bgg_ref, c_ref, s1_ref, s2_ref,
                 q_ref, kt_ref, vt_ref, gq_ref, gk_ref, gv_ref, la_ref, rg_ref, qm_ref):
    h = _layer_norm(x_ref[0], g_ref[...], b_ref[...]).astype(BF16)

    def mm(lo, hi):
        return jnp.dot(h, w_ref[:, lo:hi], preferred_element_type=F32)

    c, s1, s2 = (jnp.tile(t[...], (1, A_W // LANES)) for t in (c_ref, s1_ref, s2_ref))
    half = ROT_DIM // 2

    def rope(t):
        return t * c + pltpu.roll(t, A_W - half, 1) * s1 + pltpu.roll(t, half, 1) * s2

    q_ref[0] = rope(mm(_C_QA, _C_KA)) * (A_DIM ** -0.5 * LOG2E)
    kt_ref[0] = rope(mm(_C_KA, _C_VA)).T
    vt_ref[0] = mm(_C_VA, _C_QG).T
    gq_ref[0] = mm(_C_QG, _C_KG)
    gk_ref[0] = mm(_C_KG, _C_VG)
    gv_ref[0] = mm(_C_VG, _C_RG)
    rg_ref[0] = mm(_C_RG, _C_QM)
    qm_ref[0] = mm(_C_QM, _C_LG) * (M_DIM ** -0.5)
    z = jnp.dot(mm(_C_LG, _C_END), wgg_ref[...], precision=HI, preferred_element_type=F32) + bgg_ref[...]
    la_ref[0] = (jnp.minimum(z, 0.0) - jnp.log(1.0 + jnp.exp(-jnp.abs(z)))) * (1.0 / G_TAU)


def _rope_tables(pos):
    half = ROT_DIM // 2
    f32 = np.float32
    inv = np.power(f32(ROPE_THETA), -np.arange(half, dtype=f32) / f32(half)).astype(f32)
    ang = pos.astype(f32)[:, None] * inv[None, :]
    cos, sin = np.cos(ang).astype(f32), np.sin(ang).astype(f32)
    n = pos.shape[0]
    one = np.ones((n, A_DIM - ROT_DIM), f32)
    zero8 = np.zeros((n, half), f32)
    zero = np.zeros((n, A_DIM - ROT_DIM), f32)
    c = np.concatenate([cos, cos, one], axis=1)
    s1 = np.concatenate([-sin, zero8, zero], axis=1)
    s2 = np.concatenate([zero8, sin, zero], axis=1)
    return tuple(jnp.asarray(np.tile(t, (1, LANES // A_DIM))) for t in (c, s1, s2))


def _project(x, pos_tables, ln_g, ln_b, w_perm, wgg, bgg, ts):
    bz, s, d = x.shape
    widths = (A_W, None, None, G_K, G_K, G_V, G_K, G_V, M_W)
    tok = lambda w: (pl.BlockSpec((1, ts, w), lambda j, b: (b, j, 0)) if w else
                     pl.BlockSpec((1, A_W, ts), lambda j, b: (b, 0, j)))
    full = lambda a: pl.BlockSpec(a.shape, lambda j, b: (0,) * a.ndim)
    tab = pl.BlockSpec((ts, LANES), lambda j, b: (j, 0))
    return pl.pallas_call(
        _proj_kernel,
        grid=(s // ts, bz),
        in_specs=[tok(d), full(ln_g), full(ln_b), full(w_perm), full(wgg), full(bgg), tab, tab, tab],
        out_specs=[tok(w) for w in widths],
        out_shape=[jax.ShapeDtypeStruct((bz, s, w) if w else (bz, A_W, s), F32) for w in widths],
        compiler_params=_cparams(("arbitrary", "arbitrary")),
        name="proj",
    )(x, ln_g, ln_b, w_perm, wgg, bgg, *pos_tables)


def _mem_kv_kernel(x_ref, w_ref, kt_ref, vt_ref):
    kv = jnp.dot(x_ref[0].astype(BF16), w_ref[...], preferred_element_type=F32)
    kt_ref[0] = kv[:, :M_W].T
    vt_ref[0] = kv[:, M_W:].T


def _mem_kv(mem, w_bf16):
    bz, m, d = mem.shape
    out = pl.BlockSpec((1, M_W, m), lambda b: (b, 0, 0))
    return pl.pallas_call(
        _mem_kv_kernel,
        grid=(bz,),
        in_specs=[pl.BlockSpec((1, m, d), lambda b: (b, 0, 0)), pl.BlockSpec(w_bf16.shape, lambda b: (0, 0))],
        out_specs=[out, out],
        out_shape=[jax.ShapeDtypeStruct((bz, M_W, m), F32)] * 2,
        compiler_params=_cparams(("arbitrary",)),
        name="mem_kv",
    )(mem, w_bf16)


_MOBA_HG = 4


_MOBA_VA = A_DIM + 16


def _moba_prompt_kernel(nblk, q_ref, kt_ref, vt_ref, o_ref, kb_scr, vt_scr, km_scr, selb_scr, acc_scr,
                        sa_scr, sb_scr, sd_scr):
    blk = MOBA_BLOCK
    hg = _MOBA_HG
    va = _MOBA_VA
    i = pl.program_id(2)

    @pl.when(i == 0)
    def _():
        lane = lax.broadcasted_iota(jnp.int32, (1, hg * A_DIM), 1)
        ones = jnp.ones((va - A_DIM, blk), BF16)
        for j in range(nblk):
            kj = kt_ref[0, :, j * blk:(j + 1) * blk].T
            kb_scr[j] = kj.astype(BF16)
            kmj = jnp.mean(kj, axis=0, keepdims=True)
            vj = vt_ref[0, :, j * blk:(j + 1) * blk].astype(BF16)
            for h in range(hg):
                km_scr[h * nblk + j:h * nblk + j + 1, :] = jnp.where(lane // A_DIM == h, kmj, 0.0)
                vt_scr[j, h * va:h * va + A_DIM, :] = vj[h * A_DIM:(h + 1) * A_DIM, :]
                vt_scr[j, h * va + A_DIM:(h + 1) * va, :] = ones

    q_t = q_ref[0].T
    row = lax.broadcasted_iota(jnp.int32, q_t.shape, 0)
    q_tb = [jnp.where((row >= A_DIM * h) & (row < A_DIM * (h + 1)), q_t, 0.0).astype(BF16) for h in range(hg)]
    g = jnp.dot(km_scr[...], q_t, precision=HI, preferred_element_type=F32).reshape(hg, nblk, blk)
    blk_iota = lax.broadcasted_iota(jnp.int32, g.shape, 1)
    cnt = jnp.zeros(g.shape, jnp.int32)
    for jp in range(nblk):
        gj = g[:, jp:jp + 1, :]
        beats = (gj > g) | ((gj == g) & (jp < blk_iota))
        cnt = cnt + jnp.where(beats, 1, 0) * (jp < i).astype(jnp.int32)
    selb_scr[...] = jnp.where((blk_iota < i) & (cnt < MOBA_TOPK), 0.0, NEG)

    def scores(j, s_ref):
        kj = kb_scr[j]
        for h in range(hg):
            s_ref[h] = jnp.dot(kj, q_tb[h], preferred_element_type=F32)

    def absorb(j, s_ref, mask, ms):
        vtj = vt_scr[j]
        new = []
        for h in range(hg):
            s = mask(h, s_ref[h])
            m_blk = jnp.max(s, axis=0, keepdims=True)
            m_new = m_blk if ms is None else jnp.maximum(ms[h], m_blk)
            pv = jnp.dot(vtj[h * va:(h + 1) * va, :], jnp.exp2(s - m_new).astype(BF16), preferred_element_type=F32)
            acc_scr[h] = pv if ms is None else jnp.exp2(ms[h] - m_new) * acc_scr[h] + pv
            new.append(m_new)
        return tuple(new)

    kpos = lax.broadcasted_iota(jnp.int32, (blk, blk), 0)
    qpos = lax.broadcasted_iota(jnp.int32, (blk, blk), 1)
    last = nblk - 1
    scores(i, sd_scr)
    scores(0, sa_scr)
    ms = absorb(i, sd_scr, lambda h, s: jnp.where(kpos <= qpos, s, NEG), None)

    def past(jj, ms):
        ja, jb = 2 * jj, 2 * jj + 1
        scores(jb, sb_scr)
        ms = absorb(ja, sa_scr, lambda h, s: s + selb_scr[h, pl.ds(ja, 1), :], ms)
        scores(jnp.minimum(ja + 2, last), sa_scr)
        return absorb(jb, sb_scr, lambda h, s: s + selb_scr[h, pl.ds(jb, 1), :], ms)

    lax.fori_loop(0, (i + 1) // 2, past, ms)
    o_t = jnp.concatenate([acc_scr[h, :A_DIM, :] / acc_scr[h, A_DIM:A_DIM + 1, :] for h in range(hg)], axis=0)
    o_ref[0] = o_t.T


def _moba_prompt(q, kt, vt):
    bz, s, _ = q.shape
    blk = MOBA_BLOCK
    nblk = s // blk
    assert nblk % 2 == 0
    w = _MOBA_HG * A_DIM
    return pl.pallas_call(
        functools.partial(_moba_prompt_kernel, nblk),
        grid=(bz, A_W // w, nblk),
        in_specs=[pl.BlockSpec((1, blk, w), lambda b, hp, i: (b, i, hp)),
                  pl.BlockSpec((1, w, s), lambda b, hp, i: (b, hp, 0)),
                  pl.BlockSpec((1, w, s), lambda b, hp, i: (b, hp, 0))],
        out_specs=pl.BlockSpec((1, blk, w), lambda b, hp, i: (b, i, hp)),
        out_shape=jax.ShapeDtypeStruct((bz, s, A_W), F32),
        scratch_shapes=[pltpu.VMEM((nblk, blk, w), BF16), pltpu.VMEM((nblk, _MOBA_HG * _MOBA_VA, blk), BF16),
                        pltpu.VMEM((_MOBA_HG * nblk, w), F32), pltpu.VMEM((_MOBA_HG, nblk, blk), F32),
                        pltpu.VMEM((_MOBA_HG, _MOBA_VA, blk), F32)]
                       + [pltpu.VMEM((_MOBA_HG, blk, blk), F32)] * 3,
        compiler_params=_cparams(("arbitrary", "arbitrary", "arbitrary")),
        name="moba_prompt",
    )(q, kt, vt)


_PG_CHUNK = 8


def _moba_sample_kernel(t_new, n_pages, page, pt_ref, q_ref, kn_ref, vn_ref, ck_hbm, cv_hbm, o_ref,
                        buf, sem, s_scr, p_scr, bias_scr):
    b = pl.program_id(0)
    nb = pl.num_programs(0)
    n_chunks = n_pages // _PG_CHUNK
    n_tasks = 2 * n_chunks
    ppb = MOBA_BLOCK // page
    n_past = n_pages // ppb
    rows = t_new * A_HEADS

    def page_copy(bb, u, pg, slot):
        src = ck_hbm if u < n_chunks else cv_hbm
        phys = pt_ref[bb, (u % n_chunks) * _PG_CHUNK + pg]
        return pltpu.make_async_copy(src.at[0, phys], buf.at[slot, pg], sem.at[slot])

    def start(bb, u, slot):
        for pg in range(_PG_CHUNK):
            page_copy(bb, u, pg, slot).start(priority=pg % 2)

    def wait(bb, u, slot):
        for pg in range(_PG_CHUNK):
            page_copy(bb, u, pg, slot).wait()

    @pl.when(b == 0)
    def _():
        start(b, 0, 0)

    lane = lax.broadcasted_iota(jnp.int32, (A_HEADS, A_W), 1)
    sub = lax.broadcasted_iota(jnp.int32, (A_HEADS, A_W), 0)
    headmask = (lane // A_DIM == sub).astype(F32)
    q = q_ref[0]
    qbd = jnp.concatenate([jnp.broadcast_to(q[t:t + 1, :], (A_HEADS, A_W)) * headmask for t in range(t_new)],
                          axis=0)
    q_hi = qbd.astype(BF16)
    q_lo = (qbd - q_hi.astype(F32)).astype(BF16)
    q_hl = jnp.concatenate([q_hi, q_lo], axis=0)

    acc = jnp.zeros((rows, A_W), F32)
    l = jnp.zeros((rows, 1), F32)
    for u in range(n_tasks):
        slot = u % 2
        if u + 1 < n_tasks:
            start(b, u + 1, 1 - slot)
        else:
            @pl.when(b + 1 < nb)
            def _():
                start(b + 1, 0, 1 - slot)
        wait(b, u, slot)
        if u < n_chunks:
            for pg in range(_PG_CHUNK):
                s2 = jnp.dot(q_hl, buf[slot, pg].astype(BF16), preferred_element_type=F32)
                s_scr[u * _PG_CHUNK + pg] = s2[:rows] + s2[rows:]
        if u == n_chunks - 1:
            col = lax.broadcasted_iota(jnp.int32, (rows, n_past), 1)
            g = jnp.zeros((rows, n_past), F32)
            for jb in range(n_past):
                blk_s = s_scr[jb * ppb]
                for e in range(1, ppb):
                    blk_s = blk_s + s_scr[jb * ppb + e]
                g = jnp.where(col == jb, jnp.sum(blk_s, axis=1, keepdims=True), g)
            cnt = jnp.zeros((rows, n_past), jnp.int32)
            for jp in range(n_past):
                gj = g[:, jp:jp + 1]
                cnt = cnt + jnp.where((gj > g) | ((gj == g) & (jp < col)), 1, 0)
            selb = jnp.where(cnt < MOBA_TOPK, 0.0, NEG)
            for jb in range(n_past):
                bias_scr[jb] = jnp.broadcast_to(selb[:, jb:jb + 1], (rows, page))
            kn, vn = kn_ref[0], vn_ref[0]
            r_t = lax.broadcasted_iota(jnp.int32, (rows, 1), 0) // A_HEADS
            s_own = [jnp.where(r_t >= t, jnp.sum(qbd * kn[t:t + 1, :], axis=1, keepdims=True), NEG)
                     for t in range(t_new)]

            def smax(pg, mx):
                sb = s_scr[pg] + bias_scr[pg // ppb]
                s_scr[pg] = sb
                return jnp.maximum(mx, sb)
            mx = lax.fori_loop(0, n_pages, smax, jnp.full((rows, page), NEG, F32), unroll=4)
            m = jnp.max(mx, axis=1, keepdims=True)
            for t in range(t_new):
                m = jnp.maximum(m, s_own[t])

            def sexp(pg, ls):
                p = jnp.exp2(s_scr[pg] - m)
                p_scr[pg] = p.astype(BF16)
                return ls + p
            ls = lax.fori_loop(0, n_pages, sexp, jnp.zeros((rows, page), F32), unroll=4)
            l = jnp.sum(ls, axis=1, keepdims=True)
            for t in range(t_new):
                p_t = jnp.exp2(s_own[t] - m)
                l = l + p_t
                acc = acc + p_t * vn[t:t + 1, :]
        if u >= n_chunks:
            for pg in range(_PG_CHUNK):
                acc = acc + lax.dot_general(p_scr[(u - n_chunks) * _PG_CHUNK + pg], buf[slot, pg].astype(BF16), NT,
                                            preferred_element_type=F32)

    o = acc / l
    out_rows = [jnp.sum(o[t * A_HEADS:(t + 1) * A_HEADS, :] * headmask, axis=0, keepdims=True)
                for t in range(t_new)]
    out_rows.append(jnp.zeros((o_ref.shape[1] - t_new, A_W), F32))
    o_ref[0] = jnp.concatenate(out_rows, axis=0)


def _moba_sample(q8, kn8, vn8, ck, cv, page_table, t_new):
    db, tp, _ = q8.shape
    n_pages = page_table.shape[1]
    page = ck.shape[3]
    assert (n_pages * page) % MOBA_BLOCK == 0, "past length must end on a MoBA block boundary"
    assert MOBA_BLOCK % page == 0 and n_pages % _PG_CHUNK == 0
    n_past = n_pages * page // MOBA_BLOCK
    assert n_past >= MOBA_TOPK
    rows = t_new * A_HEADS
    tok = pl.BlockSpec((1, tp, A_W), lambda b, pt: (b, 0, 0))
    any_spec = pl.BlockSpec(memory_space=pl.ANY)
    grid_spec = pltpu.PrefetchScalarGridSpec(
        num_scalar_prefetch=1,
        grid=(db,),
        in_specs=[tok, tok, tok, any_spec, any_spec],
        out_specs=tok,
        scratch_shapes=[pltpu.VMEM((2, _PG_CHUNK, A_W, page), F32), pltpu.SemaphoreType.DMA((2,)),
                        pltpu.VMEM((n_pages, rows, page), F32), pltpu.VMEM((n_pages, rows, page), BF16),
                        pltpu.VMEM((n_past, rows, page), F32)],
    )
    return pl.pallas_call(
        functools.partial(_moba_sample_kernel, t_new, n_pages, page),
        grid_spec=grid_spec,
        out_shape=jax.ShapeDtypeStruct((db, tp, A_W), F32),
        compiler_params=_cparams(("arbitrary",)),
        name="moba_sample",
    )(page_table, q8, kn8, vn8, ck, cv)


def _gla_kernel(c, nch, carry, q_ref, k_ref, la_ref, v_ref, rg_ref, s0_ref, gg_ref, l_ref, e_ref, hm_ref,
                bd_ref, hv_ref, o_ref, sf_ref, st_scr, b_scr, qs_scr, k_scr, qb_scr, dch_scr, oo_scr):
    j = pl.program_id(1)
    if carry:
        @pl.when(j == 0)
        def _():
            st_scr[...] = s0_ref[0]

    la = la_ref[0]
    b = jnp.dot(l_ref[...], la, precision=HI, preferred_element_type=F32)
    tot = jnp.dot(e_ref[...], la, precision=HI, preferred_element_type=F32)
    qs = q_ref[0] * (G_DK ** -0.5)
    k = k_ref[0]
    b_scr[...] = b
    qs_scr[...] = qs
    k_scr[...] = k
    qb_scr[...] = qs * jnp.exp(b)
    dch_scr[...] = jnp.exp(tot)
    kdec = (k * jnp.exp(tot - b)).astype(BF16)
    v_t = v_ref[0].T
    lane_chunk = lax.broadcasted_iota(jnp.int32, v_t.shape, 1) // c
    jrow = lax.broadcasted_iota(jnp.int32, (c, G_K), 0)

    def chunk(ci, st):
        r0 = pl.multiple_of(ci * c, c)
        if not carry:
            st = s0_ref[ci]
        o_inter = lax.dot_general(qb_scr[pl.ds(r0, c), :].astype(BF16), st.astype(BF16), NT,
                                  preferred_element_type=F32)
        b_c, k_c, qs_c = b_scr[pl.ds(r0, c), :], k_scr[pl.ds(r0, c), :], qs_scr[pl.ds(r0, c), :]
        terms = []
        for i in range(c):
            d = jnp.where(jrow <= i, b_c[i:i + 1, :] - b_c, NEG)
            terms.append(qs_c[i:i + 1, :] * k_c * jnp.exp(d))
        t_all = jnp.concatenate(terms, axis=0).astype(BF16)
        att = jnp.dot(t_all, hm_ref[...], preferred_element_type=F32)
        v_c = v_ref[0, pl.ds(r0, c), :]
        o_intra = jnp.sum(att.reshape(c, c, G_V) * v_c[None, :, :], axis=1)
        oo_scr[pl.ds(r0, c), :] = o_inter + o_intra
        v_m = jnp.where(lane_chunk == ci, v_t, 0.0).astype(BF16)
        upd = jnp.dot(v_m, kdec, preferred_element_type=F32) * bd_ref[...]
        st_new = st * dch_scr[pl.ds(r0, 1), :] + upd
        if not carry:
            sf_ref[ci] = st_new
        return st_new

    st = lax.fori_loop(0, nch, chunk, st_scr[...] if carry else jnp.zeros(st_scr.shape, F32))
    if carry:
        st_scr[...] = st

        @pl.when(j == pl.num_programs(1) - 1)
        def _():
            sf_ref[0] = st

    o = oo_scr[...]
    ms = jnp.dot(o * o, hv_ref[...], precision=HI, preferred_element_type=F32)
    rg = rg_ref[0]
    o_ref[0] = o * lax.rsqrt(ms + RMS_EPS) * gg_ref[...] * (rg / (1.0 + jnp.exp(-rg)))


def _gla(gq, gk, la, gv, rg, s0_t, gg, c, nch, carry):
    bz, t, _ = gq.shape
    tt = c * nch
    idx = jnp.arange(tt)
    same = (idx[:, None] // c) == (idx[None, :] // c)
    l_mat = (same & (idx[None, :] <= idx[:, None])).astype(F32)
    e_mat = same.astype(F32)
    hm = ((jnp.arange(G_K)[:, None] // G_DK) == (jnp.arange(G_V)[None, :] // G_DV)).astype(BF16)
    bd = ((jnp.arange(G_V)[:, None] // G_DV) == (jnp.arange(G_K)[None, :] // G_DK)).astype(F32)
    hv = ((jnp.arange(G_V)[:, None] // G_DV) == (jnp.arange(G_V)[None, :] // G_DV)).astype(F32) / G_DV
    tok = lambda w: pl.BlockSpec((1, tt, w), lambda b, j: (b, j, 0))
    full = lambda a: pl.BlockSpec(a.shape, lambda b, j: (0,) * a.ndim)
    ns = 1 if carry else nch
    st_spec = pl.BlockSpec((ns, G_V, G_K), lambda b, j: (b, 0, 0))
    return pl.pallas_call(
        functools.partial(_gla_kernel, c, nch, carry),
        grid=(bz, t // tt),
        in_specs=[tok(G_K), tok(G_K), tok(G_K), tok(G_V), tok(G_V), st_spec, full(gg), full(l_mat), full(e_mat),
                  full(hm), full(bd), full(hv)],
        out_specs=[tok(G_V), st_spec],
        out_shape=[jax.ShapeDtypeStruct((bz, t, G_V), F32), jax.ShapeDtypeStruct(s0_t.shape, F32)],
        scratch_shapes=[pltpu.VMEM((G_V, G_K), F32)] + [pltpu.VMEM((tt, G_K), F32)] * 5
                       + [pltpu.VMEM((tt, G_V), F32)],
        compiler_params=_cparams(("arbitrary", "arbitrary")),
        name="gla",
    )(gq, gk, la, gv, rg, s0_t, gg, l_mat, e_mat, hm, bd, hv)


def _state_to_t(s):
    n = s.shape[0]
    eye = jnp.eye(G_HEADS, dtype=s.dtype)
    return jnp.einsum("nhdv,hg->nhvgd", s, eye).reshape(n, G_V, G_K)


def _state_from_t(st):
    n = st.shape[0]
    s5 = st.reshape(n, G_HEADS, G_DV, G_HEADS, G_DK)
    return jnp.stack([s5[:, h, :, h, :] for h in range(G_HEADS)], axis=1).transpose(0, 1, 3, 2)


def _mem_attn_kernel(q_ref, mkt_ref, mvt_ref, o_ref):
    q = q_ref[0]
    mkt = mkt_ref[0].astype(BF16)
    mvt = mvt_ref[0].astype(BF16)
    lane = lax.broadcasted_iota(jnp.int32, q.shape, 1)
    out = jnp.zeros(q.shape, F32)
    for h in range(M_HEADS):
        hsel = (lane // M_DIM) == h
        s = jnp.dot(jnp.where(hsel, q, 0.0).astype(BF16), mkt, preferred_element_type=F32)
        m = jnp.max(s, axis=1, keepdims=True)
        p = jnp.exp(s - m)
        l = jnp.sum(p, axis=1, keepdims=True)
        o_h = lax.dot_general(p.astype(BF16), mvt, NT, preferred_element_type=F32)
        out = out + jnp.where(hsel, o_h / l, 0.0)
    o_ref[0] = out


def _mem_attn(qm, mkt, mvt, tm):
    bz, t, _ = qm.shape
    mem = mkt.shape[2]
    return pl.pallas_call(
        _mem_attn_kernel,
        grid=(bz, t // tm),
        in_specs=[pl.BlockSpec((1, tm, M_W), lambda b, j: (b, j, 0)),
                  pl.BlockSpec((1, M_W, mem), lambda b, j: (b, 0, 0)),
                  pl.BlockSpec((1, M_W, mem), lambda b, j: (b, 0, 0))],
        out_specs=pl.BlockSpec((1, tm, M_W), lambda b, j: (b, j, 0)),
        out_shape=jax.ShapeDtypeStruct((bz, t, M_W), F32),
        compiler_params=_cparams(("arbitrary", "arbitrary")),
        name="mem_attn",
    )(qm, mkt, mvt)


def _mix_out_kernel(x_ref, oa_ref, og_ref, om_ref, lng_ref, lnb_ref, wo_ref, g1_ref, b1_ref, wr_ref, br_ref,
                    tri_ref, h1_ref, rt_ref, cnt_ref, carry_scr):
    @pl.when(pl.program_id(0) == 0)
    def _():
        carry_scr[...] = jnp.zeros(carry_scr.shape, F32)

    h = _layer_norm(x_ref[...], lng_ref[...], lnb_ref[...])
    o = jnp.dot(oa_ref[...].astype(BF16), wo_ref[0:A_W, :], preferred_element_type=F32)
    o = o + jnp.dot(og_ref[...].astype(BF16), wo_ref[A_W:A_W + G_V, :], preferred_element_type=F32)
    o = o + jnp.dot(om_ref[...].astype(BF16), wo_ref[A_W + G_V:, :], preferred_element_type=F32)
    h1 = _layer_norm(DEEPNORM_ALPHA * h + o, g1_ref[...], b1_ref[...])
    h1_ref[...] = h1
    work = jnp.dot(h1, wr_ref[...], precision=HI, preferred_element_type=F32) + br_ref[...]
    lane = lax.broadcasted_iota(jnp.int32, work.shape, 1).astype(F32)
    onehot = jnp.zeros(work.shape, F32)
    vals, eids = [], []
    for _ in range(TOP_K):
        mk = jnp.max(work, axis=1, keepdims=True)
        ek = jnp.min(jnp.where(work == mk, lane, float(LANES)), axis=1, keepdims=True)
        hit = lane == ek
        onehot = jnp.where(hit, 1.0, onehot)
        work = jnp.where(hit, -jnp.inf, work)
        vals.append(mk)
        eids.append(ek)
    ex = [jnp.exp(v - vals[0]) for v in vals]
    den = ex[0] + ex[1] + ex[2] + ex[3]
    pos_full = jnp.dot(tri_ref[...], onehot.astype(BF16), preferred_element_type=F32) + carry_scr[...]
    carry_scr[...] = carry_scr[...] + jnp.sum(onehot, axis=0, keepdims=True)
    cnt_ref[...] = carry_scr[...]
    rt = jnp.zeros(work.shape, F32)
    for k in range(TOP_K):
        pk = jnp.sum(jnp.where(lane == eids[k], pos_full, 0.0), axis=1, keepdims=True)
        rt = jnp.where(lane == RT_GATE + k, ex[k] / den, rt)
        rt = jnp.where(lane == RT_EID + k, eids[k], rt)
        rt = jnp.where(lane == RT_POS + k, pk, rt)
    rt_ref[...] = rt


def _mix_out(x, oa, og, om, ln_g, ln_b, wo_bf16, g1, b1, wr_pad, br_pad, tm):
    n, d = x.shape
    tri = (jnp.arange(tm)[None, :] < jnp.arange(tm)[:, None]).astype(BF16)
    tok = lambda w: pl.BlockSpec((tm, w), lambda i: (i, 0))
    full = lambda a: pl.BlockSpec(a.shape, lambda i: (0,) * a.ndim)
    return pl.pallas_call(
        _mix_out_kernel,
        grid=(n // tm,),
        in_specs=[tok(d), tok(A_W), tok(G_V), tok(M_W), full(ln_g), full(ln_b), full(wo_bf16), full(g1), full(b1),
                  full(wr_pad), full(br_pad), full(tri)],
        out_specs=[tok(d), tok(LANES), pl.BlockSpec((1, LANES), lambda i: (0, 0))],
        out_shape=[jax.ShapeDtypeStruct((n, d), F32), jax.ShapeDtypeStruct((n, LANES), F32),
                   jax.ShapeDtypeStruct((1, LANES), F32)],
        scratch_shapes=[pltpu.VMEM((1, LANES), F32)],
        compiler_params=_cparams(("arbitrary",)),
        name="mix_out",
    )(x, oa, og, om, ln_g, ln_b, wo_bf16, g1, b1, wr_pad, br_pad, tri)


def _row_copy(src, s_row, dst, d_row, sem):
    return pltpu.make_async_copy(src.at[pl.ds(s_row, 1)], dst.at[pl.ds(d_row, 1)], sem)


def _dispatch_kernel(zs_ref, ze_ref, dest_ref, h1_ref, xs_hbm, zero_scr, sems):
    tm = h1_ref.shape[0]
    sem, zsem = sems.at[0], sems.at[1]

    def issue(r, c):
        for k in range(TOP_K):
            _row_copy(h1_ref, r, xs_hbm, dest_ref[r * TOP_K + k], sem).start(priority=k % 2)
        return c
    lax.fori_loop(0, tm, issue, 0, unroll=4)

    @pl.when(pl.program_id(0) == pl.num_programs(0) - 1)
    def _():
        zero_scr[...] = jnp.zeros(zero_scr.shape, F32)

        def zero_rows(wait):
            def per_row(r, c):
                cp = _row_copy(zero_scr, 0, xs_hbm, r, zsem)
                if wait:
                    cp.wait()
                else:
                    cp.start()
                return c

            def per_expert(e, c):
                return lax.fori_loop(zs_ref[e], ze_ref[e], per_row, c)
            lax.fori_loop(0, N_EXPERTS, per_expert, 0)
        zero_rows(False)
        zero_rows(True)

    def drain(r, c):
        for k in range(TOP_K):
            _row_copy(h1_ref, r, xs_hbm, 0, sem).wait()
        return c
    lax.fori_loop(0, tm, drain, 0, unroll=4)


def _dispatch(h1, dest_flat, z_start, z_end, n_rows, tm):
    n, d = h1.shape
    grid_spec = pltpu.PrefetchScalarGridSpec(
        num_scalar_prefetch=2,
        grid=(n // tm,),
        in_specs=[pl.BlockSpec((tm * TOP_K,), lambda i, zs, ze: (i,), memory_space=pltpu.SMEM),
                  pl.BlockSpec((tm, d), lambda i, zs, ze: (i, 0))],
        out_specs=pl.BlockSpec(memory_space=pl.ANY),
        scratch_shapes=[pltpu.VMEM((8, d), F32), pltpu.SemaphoreType.DMA((2,))],
    )
    return pl.pallas_call(
        _dispatch_kernel,
        grid_spec=grid_spec,
        out_shape=jax.ShapeDtypeStruct((n_rows, d), F32),
        compiler_params=_cparams(("arbitrary",)),
        name="moe_dispatch",
    )(z_start, z_end, dest_flat, h1)


_FF_CHUNK = 512


def _moe_kernel(be_ref, nu_ref, x_ref, wg_ref, bg_ref, wu_ref, bu_ref, wd_ref, bd_ref, y_ref,
                wgb_scr, wub_scr, wdb_scr):
    i = pl.program_id(0)
    prev = be_ref[jnp.maximum(i - 1, 0)]

    @pl.when((i == 0) | (be_ref[i] != prev))
    def _():
        wgb_scr[...] = wg_ref[0].astype(BF16)
        wub_scr[...] = wu_ref[0].astype(BF16)
        wdb_scr[...] = wd_ref[0].astype(BF16)

    @pl.when(i < nu_ref[0])
    def _():
        x = x_ref[...].astype(BF16)
        d_ff = wgb_scr.shape[1]
        y = jnp.zeros(y_ref.shape, F32) + bd_ref[0]
        for f0 in range(0, d_ff, _FF_CHUNK):
            f1 = f0 + _FF_CHUNK
            g = jnp.dot(x, wgb_scr[:, f0:f1], preferred_element_type=F32) + bg_ref[0, :, f0:f1]
            u = jnp.dot(x, wub_scr[:, f0:f1], preferred_element_type=F32) + bu_ref[0, :, f0:f1]
            g = jnp.minimum(g, SWIGLU_LIMIT)
            u = jnp.clip(u, -SWIGLU_LIMIT, SWIGLU_LIMIT)
            act = g * (1.0 / (1.0 + jnp.exp(-SWIGLU_ALPHA * g))) * (u + 1.0)
            y = y + jnp.dot(act.astype(BF16), wdb_scr[f0:f1, :], preferred_element_type=F32)
        y_ref[...] = y

    @pl.when(i >= nu_ref[0])
    def _():
        y_ref[...] = jnp.zeros(y_ref.shape, F32)


def _moe_experts(xs, block_e, n_used, wg, bg, wu, bu, wd, bd, tm):
    rows, d = xs.shape
    n_blocks = rows // tm
    d_ff = wg.shape[2]
    wspec = lambda a: pl.BlockSpec((1,) + a.shape[1:], lambda i, be, nu: (be[i], 0, 0))
    grid_spec = pltpu.PrefetchScalarGridSpec(
        num_scalar_prefetch=2,
        grid=(n_blocks,),
        in_specs=[pl.BlockSpec((tm, d), lambda i, be, nu: (jnp.minimum(i, nu[0] - 1), 0)),
                  wspec(wg), wspec(bg), wspec(wu), wspec(bu), wspec(wd), wspec(bd)],
        out_specs=pl.BlockSpec((tm, d), lambda i, be, nu: (i, 0)),
        scratch_shapes=[pltpu.VMEM((d, d_ff), BF16), pltpu.VMEM((d, d_ff), BF16), pltpu.VMEM((d_ff, d), BF16)],
    )
    return pl.pallas_call(
        _moe_kernel,
        grid_spec=grid_spec,
        out_shape=jax.ShapeDtypeStruct((rows, d), F32),
        compiler_params=_cparams(("arbitrary",)),
        name="moe_experts",
    )(block_e, n_used, xs, wg, bg, wu, bu, wd, bd)


def _combine_kernel(nt_a, dcur_ref, dnxt_ref, h1_ref, rt_ref, g_ref, b_ref, yb_hbm, oa_ref, ob_ref, gbuf, sem):
    i = pl.program_id(0)
    tm = h1_ref.shape[0]

    def gather(dref, slot, wait):
        def body(r, c):
            for k in range(TOP_K):
                cp = pltpu.make_async_copy(yb_hbm.at[pl.ds(dref[r * TOP_K + k], 1)],
                                           gbuf.at[slot, k, pl.ds(r, 1)], sem.at[slot])
                if wait:
                    cp.wait()
                else:
                    cp.start(priority=k % 2)
            return c
        lax.fori_loop(0, tm, body, 0, unroll=4)

    @pl.when(i == 0)
    def _():
        gather(dcur_ref, 0, False)

    @pl.when(i + 1 < pl.num_programs(0))
    def _():
        gather(dnxt_ref, (i + 1) % 2, False)

    slot = i % 2
    gather(dcur_ref, slot, True)
    rt = rt_ref[...]
    f = rt[:, RT_GATE:RT_GATE + 1] * gbuf[slot, 0]
    for k in range(1, TOP_K):
        f = f + rt[:, RT_GATE + k:RT_GATE + k + 1] * gbuf[slot, k]
    y = _layer_norm(DEEPNORM_ALPHA * h1_ref[...] + f, g_ref[...], b_ref[...])

    @pl.when(i < nt_a)
    def _():
        oa_ref[...] = y

    @pl.when(i >= nt_a)
    def _():
        ob_ref[...] = y


def _combine_ln2(h1, rt, dest_flat, yb, g, b, n_a, tm):
    n, d = h1.shape
    nt = n // tm
    nt_a = n_a // tm
    assert n_a % tm == 0 and 0 < nt_a < nt
    tok = lambda w: pl.BlockSpec((tm, w), lambda i: (i, 0))
    full = lambda a: pl.BlockSpec(a.shape, lambda i: (0,) * a.ndim)
    return pl.pallas_call(
        functools.partial(_combine_kernel, nt_a),
        grid=(nt,),
        in_specs=[pl.BlockSpec((tm * TOP_K,), lambda i: (i,), memory_space=pltpu.SMEM),
                  pl.BlockSpec((tm * TOP_K,), lambda i: (jnp.minimum(i + 1, nt - 1),), memory_space=pltpu.SMEM),
                  tok(d), tok(LANES), full(g), full(b), pl.BlockSpec(memory_space=pl.ANY)],
        out_specs=[pl.BlockSpec((tm, d), lambda i: (jnp.minimum(i, nt_a - 1), 0)),
                   pl.BlockSpec((tm, d), lambda i: (jnp.maximum(i - nt_a, 0), 0))],
        out_shape=[jax.ShapeDtypeStruct((n_a, d), F32), jax.ShapeDtypeStruct((n - n_a, d), F32)],
        scratch_shapes=[pltpu.VMEM((2, TOP_K, tm, d), F32), pltpu.SemaphoreType.DMA((2,))],
        compiler_params=_cparams(("arbitrary",)),
        name="moe_combine_ln2",
    )(dest_flat, dest_flat, h1, rt, g, b, yb)


def _layout(rt, counts, n, tm):
    counts = counts.astype(jnp.int32)
    padded = (counts + tm - 1) // tm * tm
    pad_end = jnp.cumsum(padded)
    pad_start = pad_end - padded
    eid = rt[:, RT_EID:RT_EID + TOP_K].astype(jnp.int32)
    pos = rt[:, RT_POS:RT_POS + TOP_K].astype(jnp.int32)
    dest = (pad_start[eid] + pos).reshape(-1)
    n_blocks = -(-(n * TOP_K) // tm) + N_EXPERTS
    first_row = jnp.arange(n_blocks, dtype=jnp.int32) * tm
    block_e = jnp.minimum(jnp.sum(pad_end[None, :] <= first_row[:, None], axis=1), N_EXPERTS - 1).astype(jnp.int32)
    n_used = (pad_end[-1] // tm).astype(jnp.int32).reshape(1)
    return dest, block_e, n_used, (pad_start + counts).astype(jnp.int32), pad_end.astype(jnp.int32), n_blocks * tm


def _hd_rows(a):
    nd = a.ndim
    t = jnp.transpose(a, tuple(range(nd - 3)) + (nd - 2, nd - 1, nd - 3))
    return t.reshape(t.shape[:-3] + (t.shape[-3] * t.shape[-2], t.shape[-1]))


def _rows_hd(at, heads):
    t = at.reshape(at.shape[:-2] + (heads, at.shape[-2] // heads, at.shape[-1]))
    nd = t.ndim
    return jnp.transpose(t, tuple(range(nd - 3)) + (nd - 1, nd - 3, nd - 2))


def kernel(x_prompt, x_sample, cache_k, cache_v, page_table, state_gla, cache_mem_k, cache_mem_v, mem_prompt,
           ln_in_g, ln_in_b, w_in, w_gla_gate, b_gla_gate, g_gla, w_mem_kv, w_out, ln1_g, ln1_b,
           w_router, b_router, w_gate, b_gate, w_up, b_up, w_down, b_down, ln2_g, ln2_b):
    assert w_in.shape[0] == DEPTH == 1
    bz, seq, d = x_prompt.shape
    db, t_new, _ = x_sample.shape
    page = cache_k.shape[2]
    past = page_table.shape[1] * page
    mem_len = mem_prompt.shape[1]
    n_p, n_s = bz * seq, db * t_new
    n = n_p + n_s
    tm = TOK_TM
    t_pad = 8
    row = lambda a: a.reshape(1, -1)

    wi = w_in[0]
    c_lg = 3 * A_W + 2 * G_K + G_V
    w_perm = jnp.concatenate([wi[:, :c_lg], wi[:, c_lg + G_LOWRANK:], wi[:, c_lg:c_lg + G_LOWRANK],
                              jnp.zeros((d, LANES - G_LOWRANK), F32)], axis=1).astype(BF16)
    wgg = jnp.zeros((LANES, G_K), F32).at[:G_LOWRANK].set(w_gla_gate[0])
    bgg = row(b_gla_gate[0])
    gg = row(jnp.tile(g_gla[0], G_HEADS))
    wo = w_out[0].astype(BF16)
    wr = jnp.zeros((d, LANES), F32).at[:, :N_EXPERTS].set(w_router[0])
    br = jnp.full((1, LANES), -jnp.inf, F32).at[0, :N_EXPERTS].set(b_router[0])
    ln_g, ln_b = row(ln_in_g), row(ln_in_b)

    tabs_p = _rope_tables(np.arange(seq))
    tabs_s = _rope_tables(np.tile(past + np.arange(t_new), db))
    qp, ktp, vtp, gqp, gkp, gvp, lap, rgp, qmp = _project(x_prompt, tabs_p, ln_g, ln_b, w_perm, wgg, bgg, tm)
    xs3 = x_sample.reshape(1, n_s, d)
    qs, kts, vts, gqs, gks, gvs, las, rgs, qms = _project(xs3, tabs_s, ln_g, ln_b, w_perm, wgg, bgg, n_s)
    ks, vs = kts[0].T, vts[0].T
    mkt_p, mvt_p = _mem_kv(mem_prompt, w_mem_kv[0].astype(BF16))

    oa_p = _moba_prompt(qp, ktp, vtp)
    s0_p = jnp.zeros((bz, G_V, G_K), F32)
    og_p, st_p = _gla(gqp, gkp, lap, gvp, rgp, s0_p, gg, GLA_CHUNK, MOBA_BLOCK // GLA_CHUNK, True)
    om_p = _mem_attn(qmp, mkt_p, mvt_p, tm)

    pad_t = lambda a: jnp.pad(a.reshape(db, t_new, -1), ((0, 0), (0, t_pad - t_new), (0, 0)))
    oa_s = _moba_sample(pad_t(qs), pad_t(ks), pad_t(vs), _hd_rows(cache_k), _hd_rows(cache_v), page_table,
                        t_new)[:, :t_new]
    seq_per_step = 16
    grp = lambda a: pad_t(a).reshape(db // seq_per_step, seq_per_step * t_pad, -1)
    og_s, st_s = _gla(grp(gqs), grp(gks), grp(las), grp(gvs), grp(rgs), _state_to_t(state_gla[0]), gg,
                      t_pad, seq_per_step, False)
    og_s = og_s.reshape(db, t_pad, G_V)[:, :t_new]
    om_s = _mem_attn(pad_t(qms), _hd_rows(cache_mem_k[0]), _hd_rows(cache_mem_v[0]), t_pad)[:, :t_new]

    cat = lambda a, b_: jnp.concatenate([a.reshape(n_p, -1), b_.reshape(n_s, -1)], axis=0)
    h1, rt, cnt = _mix_out(cat(x_prompt, x_sample), cat(oa_p, oa_s), cat(og_p, og_s), cat(om_p, om_s), ln_g, ln_b,
                           wo, row(ln1_g[0]), row(ln1_b[0]), wr, br, tm)

    dest, block_e, n_used, z_start, z_end, n_rows = _layout(rt, cnt[0, :N_EXPERTS], n, tm)
    xs_sorted = _dispatch(h1, dest, z_start, z_end, n_rows, tm)
    b3 = lambda a: a[0].reshape(N_EXPERTS, 1, -1)
    yb = _moe_experts(xs_sorted, block_e, n_used, w_gate[0], b3(b_gate), w_up[0], b3(b_up), w_down[0], b3(b_down), tm)
    y_p, y_s = _combine_ln2(h1, rt, dest, yb, row(ln2_g[0]), row(ln2_b[0]), n_p, tm)

    return (y_p.reshape(bz, seq, d), y_s.reshape(db, t_new, d),
            _rows_hd(ktp, A_HEADS)[None], _rows_hd(vtp, A_HEADS)[None],
            _state_from_t(st_p)[None],
            _rows_hd(mkt_p, M_HEADS)[None], _rows_hd(mvt_p, M_HEADS)[None],
            ks.reshape(1, db, t_new, A_HEADS, A_DIM), vs.reshape(1, db, t_new, A_HEADS, A_DIM),
            _state_from_t(st_s)[None])
```

```python
import functools

import jax
import jax.numpy as jnp
import numpy as np
from jax import lax
from jax.experimental import pallas as pl
from jax.experimental.pallas import tpu as pltpu

F32 = jnp.float32
BF16 = jnp.bfloat16
HI = lax.Precision.HIGHEST
NT = (((1,), (1,)), ((), ()))

A_HEADS, A_DIM = 8, 64
ROT_DIM = A_DIM // 4
ROPE_THETA = 500000.0
MOBA_BLOCK, MOBA_TOPK = 256, 3
G_HEADS, G_DK, G_DV, G_LOWRANK, G_TAU = 4, 32, 64, 16, 16.0
M_HEADS, M_DIM = 4, 64
A_W = A_HEADS * A_DIM
G_K = G_HEADS * G_DK
G_V = G_HEADS * G_DV
M_W = M_HEADS * M_DIM
N_EXPERTS, TOP_K = 32, 4
SWIGLU_ALPHA, SWIGLU_LIMIT = 1.702, 7.0
LN_EPS, RMS_EPS = 1e-5, 1e-6
NEG = -1e30
LOG2E = 1.4426950408889634
DEPTH = 1
DEEPNORM_ALPHA = (2 * DEPTH) ** 0.25

LANES = 128
VMEM_LIMIT = 56 * 1024 * 1024

TOK_TM = 512
GLA_CHUNK = 16
RT_GATE, RT_EID, RT_POS = 0, TOP_K, 2 * TOP_K


def _cparams(sem):
    return pltpu.CompilerParams(dimension_semantics=sem, vmem_limit_bytes=VMEM_LIMIT)


def _layer_norm(x, g, b):
    mu = jnp.mean(x, axis=-1, keepdims=True)
    xc = x - mu
    var = jnp.mean(xc * xc, axis=-1, keepdims=True)
    return xc * lax.rsqrt(var + LN_EPS) * g + b


_C_QA, _C_KA, _C_VA = 0, A_W, 2 * A_W
_C_QG = 3 * A_W
_C_KG = _C_QG + G_K
_C_VG = _C_KG + G_K
_C_RG = _C_VG + G_V
_C_QM = _C_RG + G_V
_C_LG = _C_QM + M_W
_C_END = _C_LG + LANES


def _proj_kernel(x_ref, g_ref, b_ref, w_ref, wgg_ref, bgg_ref, c_ref, s1_ref, s2_ref,
                 q_ref, kt_ref, vt_ref, gq_ref, gk_ref, gv_ref, la_ref, rg_ref, qm_ref):
    h = _layer_norm(x_ref[0], g_ref[...], b_ref[...]).astype(BF16)

    def mm(lo, hi):
        return jnp.dot(h, w_ref[:, lo:hi], preferred_element_type=F32)

    c, s1, s2 = (jnp.tile(t[...], (1, A_W // LANES)) for t in (c_ref, s1_ref, s2_ref))
    half = ROT_DIM // 2

    def rope(t):
        return t * c + pltpu.roll(t, A_W - half, 1) * s1 + pltpu.roll(t, half, 1) * s2

    q_ref[0] = rope(mm(_C_QA, _C_KA)) * (A_DIM ** -0.5 * LOG2E)
    kt_ref[0] = rope(mm(_C_KA, _C_VA)).T
    vt_ref[0] = mm(_C_VA, _C_QG).T
    gq_ref[0] = mm(_C_QG, _C_KG)
    gk_ref[0] = mm(_C_KG, _C_VG)
    gv_ref[0] = mm(_C_VG, _C_RG)
    rg_ref[0] = mm(_C_RG, _C_QM)
    qm_ref[0] = mm(_C_QM, _C_LG) * (M_DIM ** -0.5)
    z = jnp.dot(mm(_C_LG, _C_END), wgg_ref[...], precision=HI, preferred_element_type=F32) + bgg_ref[...]
    la_ref[0] = (jnp.minimum(z, 0.0) - jnp.log(1.0 + jnp.exp(-jnp.abs(z)))) * (1.0 / G_TAU)


def _rope_tables(pos):
    half = ROT_DIM // 2
    f32 = np.float32
    inv = np.power(f32(ROPE_THETA), -np.arange(half, dtype=f32) / f32(half)).astype(f32)
    ang = pos.astype(f32)[:, None] * inv[None, :]
    cos, sin = np.cos(ang).astype(f32), np.sin(ang).astype(f32)
    n = pos.shape[0]
    one = np.ones((n, A_DIM - ROT_DIM), f32)
    zero8 = np.zeros((n, half), f32)
    zero = np.zeros((n, A_DIM - ROT_DIM), f32)
    c = np.concatenate([cos, cos, one], axis=1)
    s1 = np.concatenate([-sin, zero8, zero], axis=1)
    s2 = np.concatenate([zero8, sin, zero], axis=1)
    return tuple(jnp.asarray(np.tile(t, (1, LANES // A_DIM))) for t in (c, s1, s2))


def _project(x, pos_tables, ln_g, ln_b, w_perm, wgg, bgg, ts):
    bz, s, d = x.shape
    widths = (A_W, None, None, G_K, G_K, G_V, G_K, G_V, M_W)
    tok = lambda w: (pl.BlockSpec((1, ts, w), lambda j, b: (b, j, 0)) if w else
                     pl.BlockSpec((1, A_W, ts), lambda j, b: (b, 0, j)))
    full = lambda a: pl.BlockSpec(a.shape, lambda j, b: (0,) * a.ndim)
    tab = pl.BlockSpec((ts, LANES), lambda j, b: (j, 0))
    return pl.pallas_call(
        _proj_kernel,
        grid=(s // ts, bz),
        in_specs=[tok(d), full(ln_g), full(ln_b), full(w_perm), full(wgg), full(bgg), tab, tab, tab],
        out_specs=[tok(w) for w in widths],
        out_shape=[jax.ShapeDtypeStruct((bz, s, w) if w else (bz, A_W, s), F32) for w in widths],
        compiler_params=_cparams(("arbitrary", "arbitrary")),
        name="proj",
    )(x, ln_g, ln_b, w_perm, wgg, bgg, *pos_tables)


def _mem_kv_kernel(x_ref, w_ref, kt_ref, vt_ref):
    kv = jnp.dot(x_ref[0].astype(BF16), w_ref[...], preferred_element_type=F32)
    kt_ref[0] = kv[:, :M_W].T
    vt_ref[0] = kv[:, M_W:].T


def _mem_kv(mem, w_bf16):
    bz, m, d = mem.shape
    out = pl.BlockSpec((1, M_W, m), lambda b: (b, 0, 0))
    return pl.pallas_call(
        _mem_kv_kernel,
        grid=(bz,),
        in_specs=[pl.BlockSpec((1, m, d), lambda b: (b, 0, 0)), pl.BlockSpec(w_bf16.shape, lambda b: (0, 0))],
        out_specs=[out, out],
        out_shape=[jax.ShapeDtypeStruct((bz, M_W, m), F32)] * 2,
        compiler_params=_cparams(("arbitrary",)),
        name="mem_kv",
    )(mem, w_bf16)


_MOBA_HG = 4


_MOBA_VA = A_DIM + 16


def _moba_prompt_kernel(nblk, q_ref, kt_ref, vt_ref, o_ref, kb_scr, vt_scr, km_scr, selb_scr, acc_scr,
                        sa_scr, sb_scr, sd_scr):
    blk = MOBA_BLOCK
    hg = _MOBA_HG
    va = _MOBA_VA
    i = pl.program_id(2)

    @pl.when(i == 0)
    def _():
        lane = lax.broadcasted_iota(jnp.int32, (1, hg * A_DIM), 1)
        ones = jnp.ones((va - A_DIM, blk), BF16)
        for j in range(nblk):
            kj = kt_ref[0, :, j * blk:(j + 1) * blk].T
            kb_scr[j] = kj.astype(BF16)
            kmj = jnp.mean(kj, axis=0, keepdims=True)
            vj = vt_ref[0, :, j * blk:(j + 1) * blk].astype(BF16)
            for h in range(hg):
                km_scr[h * nblk + j:h * nblk + j + 1, :] = jnp.where(lane // A_DIM == h, kmj, 0.0)
                vt_scr[j, h * va:h * va + A_DIM, :] = vj[h * A_DIM:(h + 1) * A_DIM, :]
                vt_scr[j, h * va + A_DIM:(h + 1) * va, :] = ones

    q_t = q_ref[0].T
    row = lax.broadcasted_iota(jnp.int32, q_t.shape, 0)
    q_tb = [jnp.where((row >= A_DIM * h) & (row < A_DIM * (h + 1)), q_t, 0.0).astype(BF16) for h in range(hg)]
    g = jnp.dot(km_scr[...], q_t, precision=HI, preferred_element_type=F32).reshape(hg, nblk, blk)
    blk_iota = lax.broadcasted_iota(jnp.int32, g.shape, 1)
    cnt = jnp.zeros(g.shape, jnp.int32)
    for jp in range(nblk):
        gj = g[:, jp:jp + 1, :]
        beats = (gj > g) | ((gj == g) & (jp < blk_iota))
        cnt = cnt + jnp.where(beats, 1, 0) * (jp < i).astype(jnp.int32)
    selb_scr[...] = jnp.where((blk_iota < i) & (cnt < MOBA_TOPK), 0.0, NEG)

    def scores(j, s_ref):
        kj = kb_scr[j]
        for h in range(hg):
            s_ref[h] = jnp.dot(kj, q_tb[h], preferred_element_type=F32)

    def absorb(j, s_ref, mask, ms):
        vtj = vt_scr[j]
        new = []
        for h in range(hg):
            s = mask(h, s_ref[h])
            m_blk = jnp.max(s, axis=0, keepdims=True)
            m_new = m_blk if ms is None else jnp.maximum(ms[h], m_blk)
            pv = jnp.dot(vtj[h * va:(h + 1) * va, :], jnp.exp2(s - m_new).astype(BF16), preferred_element_type=F32)
            acc_scr[h] = pv if ms is None else jnp.exp2(ms[h] - m_new) * acc_scr[h] + pv
            new.append(m_new)
        return tuple(new)

    kpos = lax.broadcasted_iota(jnp.int32, (blk, blk), 0)
    qpos = lax.broadcasted_iota(jnp.int32, (blk, blk), 1)
    last = nblk - 1
    scores(i, sd_scr)
    scores(0, sa_scr)
    ms = absorb(i, sd_scr, lambda h, s: jnp.where(kpos <= qpos, s, NEG), None)

    def past(jj, ms):
        ja, jb = 2 * jj, 2 * jj + 1
        scores(jb, sb_scr)
        ms = absorb(ja, sa_scr, lambda h, s: s + selb_scr[h, pl.ds(ja, 1), :], ms)
        scores(jnp.minimum(ja + 2, last), sa_scr)
        return absorb(jb, sb_scr, lambda h, s: s + selb_scr[h, pl.ds(jb, 1), :], ms)

    lax.fori_loop(0, (i + 1) // 2, past, ms)
    o_t = jnp.concatenate([acc_scr[h, :A_DIM, :] / acc_scr[h, A_DIM:A_DIM + 1, :] for h in range(hg)], axis=0)
    o_ref[0] = o_t.T


def _moba_prompt(q, kt, vt):
    bz, s, _ = q.shape
    blk = MOBA_BLOCK
    nblk = s // blk
    assert nblk % 2 == 0
    w = _MOBA_HG * A_DIM
    return pl.pallas_call(
        functools.partial(_moba_prompt_kernel, nblk),
        grid=(bz, A_W // w, nblk),
        in_specs=[pl.BlockSpec((1, blk, w), lambda b, hp, i: (b, i, hp)),
                  pl.BlockSpec((1, w, s), lambda b, hp, i: (b, hp, 0)),
                  pl.BlockSpec((1, w, s), lambda b, hp, i: (b, hp, 0))],
        out_specs=pl.BlockSpec((1, blk, w), lambda b, hp, i: (b, i, hp)),
        out_shape=jax.ShapeDtypeStruct((bz, s, A_W), F32),
        scratch_shapes=[pltpu.VMEM((nblk, blk, w), BF16), pltpu.VMEM((nblk, _MOBA_HG * _MOBA_VA, blk), BF16),
                        pltpu.VMEM((_MOBA_HG * nblk, w), F32), pltpu.VMEM((_MOBA_HG, nblk, blk), F32),
                        pltpu.VMEM((_MOBA_HG, _MOBA_VA, blk), F32)]
                       + [pltpu.VMEM((_MOBA_HG, blk, blk), F32)] * 3,
        compiler_params=_cparams(("arbitrary", "arbitrary", "arbitrary")),
        name="moba_prompt",
    )(q, kt, vt)


_PG_CHUNK = 8
_K_SLOTS = 4


def _moba_sample_kernel(t_new, n_pages, page, pt_ref, q_ref, kn_ref, vn_ref, ck_hbm, cv_hbm, o_ref,
                        kbuf, vbuf, ksem, vsem, fsem, s_scr, p_scr, bias_scr, flag_v, flag_s, unit_s):
    b = pl.program_id(0)
    nb = pl.num_programs(0)
    n_chunks = n_pages // _PG_CHUNK
    ppb = MOBA_BLOCK // page
    n_past = n_pages // ppb
    rows = t_new * A_HEADS

    def k_copy(bb, u, pg):
        slot = u % _K_SLOTS
        return pltpu.make_async_copy(ck_hbm.at[0, pt_ref[bb, u * _PG_CHUNK + pg]], kbuf.at[slot, pg], ksem.at[slot])

    def k_start(bb, u):
        for pg in range(_PG_CHUNK):
            k_copy(bb, u, pg).start()

    @pl.when(b == 0)
    def _():
        for u in range(_K_SLOTS):
            k_start(b, u)

        def zero(pg, c):
            vbuf[pg] = jnp.zeros(vbuf.shape[1:], F32)
            return c
        lax.fori_loop(0, n_pages, zero, 0)

    lane = lax.broadcasted_iota(jnp.int32, (A_HEADS, A_W), 1)
    sub = lax.broadcasted_iota(jnp.int32, (A_HEADS, A_W), 0)
    headmask = (lane // A_DIM == sub).astype(F32)
    q = q_ref[0]
    qbd = jnp.concatenate([jnp.broadcast_to(q[t:t + 1, :], (A_HEADS, A_W)) * headmask for t in range(t_new)],
                          axis=0)
    q_hi = qbd.astype(BF16)
    q_lo = (qbd - q_hi.astype(F32)).astype(BF16)
    q_hl = jnp.concatenate([q_hi, q_lo], axis=0)

    for u in range(n_chunks):
        for pg in range(_PG_CHUNK):
            k_copy(b, u, pg).wait()
        for pg in range(_PG_CHUNK):
            s2 = jnp.dot(q_hl, kbuf[u % _K_SLOTS, pg].astype(BF16), preferred_element_type=F32)
            s_scr[u * _PG_CHUNK + pg] = s2[:rows] + s2[rows:]
        nxt = u + _K_SLOTS
        if nxt < n_chunks:
            k_start(b, nxt)
        else:
            @pl.when(b + 1 < nb)
            def _():
                k_start(b + 1, nxt - n_chunks)

    col = lax.broadcasted_iota(jnp.int32, (rows, n_past), 1)
    g = jnp.zeros((rows, n_past), F32)
    for jb in range(n_past):
        blk_s = s_scr[jb * ppb]
        for e in range(1, ppb):
            blk_s = blk_s + s_scr[jb * ppb + e]
        g = jnp.where(col == jb, jnp.sum(blk_s, axis=1, keepdims=True), g)
    cnt = jnp.zeros((rows, n_past), jnp.int32)
    for jp in range(n_past):
        gj = g[:, jp:jp + 1]
        cnt = cnt + jnp.where((gj > g) | ((gj == g) & (jp < col)), 1, 0)
    sel = cnt < MOBA_TOPK

    picked = jnp.where(sel, 1, 0)
    head_any = picked[0:A_HEADS]
    for t in range(1, t_new):
        head_any = jnp.maximum(head_any, picked[t * A_HEADS:(t + 1) * A_HEADS])
    flag_v[...] = jnp.zeros(flag_v.shape, jnp.int32)
    flag_v[:, 0:n_past] = head_any
    flag_copy = pltpu.make_async_copy(flag_v, flag_s, fsem)
    flag_copy.start()
    flag_copy.wait()

    def collect(jb, n_units):
        for h in range(A_HEADS):
            unit_s[n_units] = jb * A_HEADS + h
            n_units = n_units + flag_s[h, jb]
        return n_units
    n_units = lax.fori_loop(0, n_past, collect, 0)

    def v_copies(idx):
        unit = unit_s[idx]
        jb = unit // A_HEADS
        r0 = pl.multiple_of((unit % A_HEADS) * A_DIM, A_DIM)
        return [pltpu.make_async_copy(cv_hbm.at[0, pt_ref[b, jb * ppb + e], pl.ds(r0, A_DIM), :],
                                      vbuf.at[jb * ppb + e, pl.ds(r0, A_DIM), :], vsem) for e in range(ppb)]

    def v_start(idx, c):
        for cp in v_copies(idx):
            cp.start()
        return c
    lax.fori_loop(0, n_units, v_start, 0)

    selb = jnp.where(sel, 0.0, NEG)
    for jb in range(n_past):
        bias_scr[jb] = jnp.broadcast_to(selb[:, jb:jb + 1], (rows, page))
    kn, vn = kn_ref[0], vn_ref[0]
    r_t = lax.broadcasted_iota(jnp.int32, (rows, 1), 0) // A_HEADS
    s_own = [jnp.where(r_t >= t, jnp.sum(qbd * kn[t:t + 1, :], axis=1, keepdims=True), NEG) for t in range(t_new)]

    def smax(pg, mx):
        sb = s_scr[pg] + bias_scr[pg // ppb]
        s_scr[pg] = sb
        return jnp.maximum(mx, sb)
    mx = lax.fori_loop(0, n_pages, smax, jnp.full((rows, page), NEG, F32), unroll=4)
    m = jnp.max(mx, axis=1, keepdims=True)
    for t in range(t_new):
        m = jnp.maximum(m, s_own[t])

    def sexp(pg, ls):
        p = jnp.exp2(s_scr[pg] - m)
        p_scr[pg] = p.astype(BF16)
        return ls + p
    ls = lax.fori_loop(0, n_pages, sexp, jnp.zeros((rows, page), F32), unroll=4)
    l = jnp.sum(ls, axis=1, keepdims=True)
    acc = jnp.zeros((rows, A_W), F32)
    for t in range(t_new):
        p_t = jnp.exp2(s_own[t] - m)
        l = l + p_t
        acc = acc + p_t * vn[t:t + 1, :]

    def v_wait(idx, c):
        for cp in v_copies(idx):
            cp.wait()
        return c
    lax.fori_loop(0, n_units, v_wait, 0)

    def v_page(pg, a):
        return a + lax.dot_general(p_scr[pg], vbuf[pg].astype(BF16), NT, preferred_element_type=F32)
    acc = lax.fori_loop(0, n_pages, v_page, acc, unroll=8)

    o = acc / l
    out_rows = [jnp.sum(o[t * A_HEADS:(t + 1) * A_HEADS, :] * headmask, axis=0, keepdims=True)
                for t in range(t_new)]
    out_rows.append(jnp.zeros((o_ref.shape[1] - t_new, A_W), F32))
    o_ref[0] = jnp.concatenate(out_rows, axis=0)


def _moba_sample(q8, kn8, vn8, ck, cv, page_table, t_new):
    db, tp, _ = q8.shape
    n_pages = page_table.shape[1]
    page = ck.shape[3]
    assert (n_pages * page) % MOBA_BLOCK == 0, "past length must end on a MoBA block boundary"
    assert MOBA_BLOCK % page == 0 and n_pages % (_PG_CHUNK * _K_SLOTS) == 0
    n_past = n_pages * page // MOBA_BLOCK
    assert MOBA_TOPK <= n_past <= LANES
    rows = t_new * A_HEADS
    tok = pl.BlockSpec((1, tp, A_W), lambda b, pt: (b, 0, 0))
    any_spec = pl.BlockSpec(memory_space=pl.ANY)
    grid_spec = pltpu.PrefetchScalarGridSpec(
        num_scalar_prefetch=1,
        grid=(db,),
        in_specs=[tok, tok, tok, any_spec, any_spec],
        out_specs=tok,
        scratch_shapes=[pltpu.VMEM((_K_SLOTS, _PG_CHUNK, A_W, page), F32), pltpu.VMEM((n_pages, A_W, page), F32),
                        pltpu.SemaphoreType.DMA((_K_SLOTS,)), pltpu.SemaphoreType.DMA(()), pltpu.SemaphoreType.DMA(()),
                        pltpu.VMEM((n_pages, rows, page), F32), pltpu.VMEM((n_pages, rows, page), BF16),
                        pltpu.VMEM((n_past, rows, page), F32),
                        pltpu.VMEM((A_HEADS, LANES), jnp.int32), pltpu.SMEM((A_HEADS, LANES), jnp.int32),
                        pltpu.SMEM((n_past * A_HEADS,), jnp.int32)],
    )
    return pl.pallas_call(
        functools.partial(_moba_sample_kernel, t_new, n_pages, page),
        grid_spec=grid_spec,
        out_shape=jax.ShapeDtypeStruct((db, tp, A_W), F32),
        compiler_params=_cparams(("arbitrary",)),
        name="moba_sample",
    )(page_table, q8, kn8, vn8, ck, cv)


def _gla_kernel(c, nch, carry, q_ref, k_ref, la_ref, v_ref, rg_ref, s0_ref, gg_ref, l_ref, e_ref, hm_ref,
                bd_ref, hv_ref, o_ref, sf_ref, st_scr, b_scr, qs_scr, k_scr, qb_scr, dch_scr, oo_scr):
    j = pl.program_id(1)
    if carry:
        @pl.when(j == 0)
        def _():
            st_scr[...] = s0_ref[0]

    la = la_ref[0]
    b = jnp.dot(l_ref[...], la, precision=HI, preferred_element_type=F32)
    tot = jnp.dot(e_ref[...], la, precision=HI, preferred_element_type=F32)
    qs = q_ref[0] * (G_DK ** -0.5)
    k = k_ref[0]
    b_scr[...] = b
    qs_scr[...] = qs
    k_scr[...] = k
    qb_scr[...] = qs * jnp.exp(b)
    dch_scr[...] = jnp.exp(tot)
    kdec = (k * jnp.exp(tot - b)).astype(BF16)
    v_t = v_ref[0].T
    lane_chunk = lax.broadcasted_iota(jnp.int32, v_t.shape, 1) // c
    jrow = lax.broadcasted_iota(jnp.int32, (c, G_K), 0)

    def chunk(ci, st):
        r0 = pl.multiple_of(ci * c, c)
        if not carry:
            st = s0_ref[ci]
        o_inter = lax.dot_general(qb_scr[pl.ds(r0, c), :].astype(BF16), st.astype(BF16), NT,
                                  preferred_element_type=F32)
        b_c, k_c, qs_c = b_scr[pl.ds(r0, c), :], k_scr[pl.ds(r0, c), :], qs_scr[pl.ds(r0, c), :]
        terms = []
        for i in range(c):
            d = jnp.where(jrow <= i, b_c[i:i + 1, :] - b_c, NEG)
            terms.append(qs_c[i:i + 1, :] * k_c * jnp.exp(d))
        t_all = jnp.concatenate(terms, axis=0).astype(BF16)
        att = jnp.dot(t_all, hm_ref[...], preferred_element_type=F32)
        v_c = v_ref[0, pl.ds(r0, c), :]
        o_intra = jnp.sum(att.reshape(c, c, G_V) * v_c[None, :, :], axis=1)
        oo_scr[pl.ds(r0, c), :] = o_inter + o_intra
        v_m = jnp.where(lane_chunk == ci, v_t, 0.0).astype(BF16)
        upd = jnp.dot(v_m, kdec, preferred_element_type=F32) * bd_ref[...]
        st_new = st * dch_scr[pl.ds(r0, 1), :] + upd
        if not carry:
            sf_ref[ci] = st_new
        return st_new

    st = lax.fori_loop(0, nch, chunk, st_scr[...] if carry else jnp.zeros(st_scr.shape, F32))
    if carry:
        st_scr[...] = st

        @pl.when(j == pl.num_programs(1) - 1)
        def _():
            sf_ref[0] = st

    o = oo_scr[...]
    ms = jnp.dot(o * o, hv_ref[...], precision=HI, preferred_element_type=F32)
    rg = rg_ref[0]
    o_ref[0] = o * lax.rsqrt(ms + RMS_EPS) * gg_ref[...] * (rg / (1.0 + jnp.exp(-rg)))


def _gla(gq, gk, la, gv, rg, s0_t, gg, c, nch, carry):
    bz, t, _ = gq.shape
    tt = c * nch
    idx = jnp.arange(tt)
    same = (idx[:, None] // c) == (idx[None, :] // c)
    l_mat = (same & (idx[None, :] <= idx[:, None])).astype(F32)
    e_mat = same.astype(F32)
    hm = ((jnp.arange(G_K)[:, None] // G_DK) == (jnp.arange(G_V)[None, :] // G_DV)).astype(BF16)
    bd = ((jnp.arange(G_V)[:, None] // G_DV) == (jnp.arange(G_K)[None, :] // G_DK)).astype(F32)
    hv = ((jnp.arange(G_V)[:, None] // G_DV) == (jnp.arange(G_V)[None, :] // G_DV)).astype(F32) / G_DV
    tok = lambda w: pl.BlockSpec((1, tt, w), lambda b, j: (b, j, 0))
    full = lambda a: pl.BlockSpec(a.shape, lambda b, j: (0,) * a.ndim)
    ns = 1 if carry else nch
    st_spec = pl.BlockSpec((ns, G_V, G_K), lambda b, j: (b, 0, 0))
    return pl.pallas_call(
        functools.partial(_gla_kernel, c, nch, carry),
        grid=(bz, t // tt),
        in_specs=[tok(G_K), tok(G_K), tok(G_K), tok(G_V), tok(G_V), st_spec, full(gg), full(l_mat), full(e_mat),
                  full(hm), full(bd), full(hv)],
        out_specs=[tok(G_V), st_spec],
        out_shape=[jax.ShapeDtypeStruct((bz, t, G_V), F32), jax.ShapeDtypeStruct(s0_t.shape, F32)],
        scratch_shapes=[pltpu.VMEM((G_V, G_K), F32)] + [pltpu.VMEM((tt, G_K), F32)] * 5
                       + [pltpu.VMEM((tt, G_V), F32)],
        compiler_params=_cparams(("arbitrary", "arbitrary")),
        name="gla",
    )(gq, gk, la, gv, rg, s0_t, gg, l_mat, e_mat, hm, bd, hv)


def _state_to_t(s):
    n = s.shape[0]
    eye = jnp.eye(G_HEADS, dtype=s.dtype)
    return jnp.einsum("nhdv,hg->nhvgd", s, eye).reshape(n, G_V, G_K)


def _state_from_t(st):
    n = st.shape[0]
    s5 = st.reshape(n, G_HEADS, G_DV, G_HEADS, G_DK)
    return jnp.stack([s5[:, h, :, h, :] for h in range(G_HEADS)], axis=1).transpose(0, 1, 3, 2)


def _mem_attn_kernel(q_ref, mkt_ref, mvt_ref, o_ref):
    q = q_ref[0]
    mkt = mkt_ref[0].astype(BF16)
    mvt = mvt_ref[0].astype(BF16)
    lane = lax.broadcasted_iota(jnp.int32, q.shape, 1)
    out = jnp.zeros(q.shape, F32)
    for h in range(M_HEADS):
        hsel = (lane // M_DIM) == h
        s = jnp.dot(jnp.where(hsel, q, 0.0).astype(BF16), mkt, preferred_element_type=F32)
        m = jnp.max(s, axis=1, keepdims=True)
        p = jnp.exp(s - m)
        l = jnp.sum(p, axis=1, keepdims=True)
        o_h = lax.dot_general(p.astype(BF16), mvt, NT, preferred_element_type=F32)
        out = out + jnp.where(hsel, o_h / l, 0.0)
    o_ref[0] = out


def _mem_attn(qm, mkt, mvt, tm):
    bz, t, _ = qm.shape
    mem = mkt.shape[2]
    return pl.pallas_call(
        _mem_attn_kernel,
        grid=(bz, t // tm),
        in_specs=[pl.BlockSpec((1, tm, M_W), lambda b, j: (b, j, 0)),
                  pl.BlockSpec((1, M_W, mem), lambda b, j: (b, 0, 0)),
                  pl.BlockSpec((1, M_W, mem), lambda b, j: (b, 0, 0))],
        out_specs=pl.BlockSpec((1, tm, M_W), lambda b, j: (b, j, 0)),
        out_shape=jax.ShapeDtypeStruct((bz, t, M_W), F32),
        compiler_params=_cparams(("arbitrary", "arbitrary")),
        name="mem_attn",
    )(qm, mkt, mvt)


def _mix_out_kernel(x_ref, oa_ref, og_ref, om_ref, lng_ref, lnb_ref, wo_ref, g1_ref, b1_ref, wr_ref, br_ref,
                    tri_ref, cnt0_ref, h1_all, rt_all, h1_ref, rt_ref, cnt_ref, carry_scr):
    del h1_all, rt_all

    @pl.when(pl.program_id(0) == 0)
    def _():
        carry_scr[...] = cnt0_ref[...]

    h = _layer_norm(x_ref[...], lng_ref[...], lnb_ref[...])
    o = jnp.dot(oa_ref[...].astype(BF16), wo_ref[0:A_W, :], preferred_element_type=F32)
    o = o + jnp.dot(og_ref[...].astype(BF16), wo_ref[A_W:A_W + G_V, :], preferred_element_type=F32)
    o = o + jnp.dot(om_ref[...].astype(BF16), wo_ref[A_W + G_V:, :], preferred_element_type=F32)
    h1 = _layer_norm(DEEPNORM_ALPHA * h + o, g1_ref[...], b1_ref[...])
    h1_ref[...] = h1
    work = jnp.dot(h1, wr_ref[...], precision=HI, preferred_element_type=F32) + br_ref[...]
    lane = lax.broadcasted_iota(jnp.int32, work.shape, 1).astype(F32)
    onehot = jnp.zeros(work.shape, F32)
    vals, eids = [], []
    for _ in range(TOP_K):
        mk = jnp.max(work, axis=1, keepdims=True)
        ek = jnp.min(jnp.where(work == mk, lane, float(LANES)), axis=1, keepdims=True)
        hit = lane == ek
        onehot = jnp.where(hit, 1.0, onehot)
        work = jnp.where(hit, -jnp.inf, work)
        vals.append(mk)
        eids.append(ek)
    ex = [jnp.exp(v - vals[0]) for v in vals]
    den = ex[0] + ex[1] + ex[2] + ex[3]
    pos_full = jnp.dot(tri_ref[...], onehot.astype(BF16), preferred_element_type=F32) + carry_scr[...]
    carry_scr[...] = carry_scr[...] + jnp.sum(onehot, axis=0, keepdims=True)
    cnt_ref[...] = carry_scr[...]
    rt = jnp.zeros(work.shape, F32)
    for k in range(TOP_K):
        pk = jnp.sum(jnp.where(lane == eids[k], pos_full, 0.0), axis=1, keepdims=True)
        rt = jnp.where(lane == RT_GATE + k, ex[k] / den, rt)
        rt = jnp.where(lane == RT_EID + k, eids[k], rt)
        rt = jnp.where(lane == RT_POS + k, pk, rt)
    rt_ref[...] = rt


def _mix_out(x, oa, og, om, ln_g, ln_b, wo_bf16, g1, b1, wr_pad, br_pad, cnt0, h1_all, rt_all, row0, tm):
    n, d = x.shape
    off = row0 // tm
    assert row0 % tm == 0 and n % tm == 0
    tri = (jnp.arange(tm)[None, :] < jnp.arange(tm)[:, None]).astype(BF16)
    tok = lambda w: pl.BlockSpec((tm, w), lambda i: (i, 0))
    out = lambda w: pl.BlockSpec((tm, w), lambda i: (i + off, 0))
    full = lambda a: pl.BlockSpec(a.shape, lambda i: (0,) * a.ndim)
    any_spec = pl.BlockSpec(memory_space=pl.ANY)
    return pl.pallas_call(
        _mix_out_kernel,
        grid=(n // tm,),
        in_specs=[tok(d), tok(A_W), tok(G_V), tok(M_W), full(ln_g), full(ln_b), full(wo_bf16), full(g1), full(b1),
                  full(wr_pad), full(br_pad), full(tri), full(cnt0), any_spec, any_spec],
        out_specs=[out(d), out(LANES), pl.BlockSpec((1, LANES), lambda i: (0, 0))],
        out_shape=[jax.ShapeDtypeStruct(h1_all.shape, F32), jax.ShapeDtypeStruct(rt_all.shape, F32),
                   jax.ShapeDtypeStruct((1, LANES), F32)],
        input_output_aliases={13: 0, 14: 1},
        scratch_shapes=[pltpu.VMEM((1, LANES), F32)],
        compiler_params=_cparams(("arbitrary",)),
        name="mix_out",
    )(x, oa, og, om, ln_g, ln_b, wo_bf16, g1, b1, wr_pad, br_pad, tri, cnt0, h1_all, rt_all)


def _row_copy(src, s_row, dst, d_row, sem):
    return pltpu.make_async_copy(src.at[pl.ds(s_row, 1)], dst.at[pl.ds(d_row, 1)], sem)


def _dispatch_kernel(zs_ref, ze_ref, dest_ref, h1_ref, xs_hbm, zero_scr, sems):
    tm = h1_ref.shape[0]
    sem, zsem = sems.at[0], sems.at[1]

    def issue(r, c):
        for k in range(TOP_K):
            _row_copy(h1_ref, r, xs_hbm, dest_ref[r * TOP_K + k], sem).start()
        return c
    lax.fori_loop(0, tm, issue, 0, unroll=4)

    @pl.when(pl.program_id(0) == pl.num_programs(0) - 1)
    def _():
        zero_scr[...] = jnp.zeros(zero_scr.shape, F32)

        def zero_rows(wait):
            def per_row(r, c):
                cp = _row_copy(zero_scr, 0, xs_hbm, r, zsem)
                if wait:
                    cp.wait()
                else:
                    cp.start()
                return c

            def per_expert(e, c):
                return lax.fori_loop(zs_ref[e], ze_ref[e], per_row, c)
            lax.fori_loop(0, N_EXPERTS, per_expert, 0)
        zero_rows(False)
        zero_rows(True)

    def drain(r, c):
        for k in range(TOP_K):
            _row_copy(h1_ref, r, xs_hbm, 0, sem).wait()
        return c
    lax.fori_loop(0, tm, drain, 0, unroll=4)


def _dispatch(h1, dest_flat, z_start, z_end, n_rows, tm):
    n, d = h1.shape
    grid_spec = pltpu.PrefetchScalarGridSpec(
        num_scalar_prefetch=2,
        grid=(n // tm,),
        in_specs=[pl.BlockSpec((tm * TOP_K,), lambda i, zs, ze: (i,), memory_space=pltpu.SMEM),
                  pl.BlockSpec((tm, d), lambda i, zs, ze: (i, 0))],
        out_specs=pl.BlockSpec(memory_space=pl.ANY),
        scratch_shapes=[pltpu.VMEM((8, d), F32), pltpu.SemaphoreType.DMA((2,))],
    )
    return pl.pallas_call(
        _dispatch_kernel,
        grid_spec=grid_spec,
        out_shape=jax.ShapeDtypeStruct((n_rows, d), F32),
        compiler_params=_cparams(("arbitrary",)),
        name="moe_dispatch",
    )(z_start, z_end, dest_flat, h1)


_FF_CHUNK = 512


def _moe_kernel(be_ref, nu_ref, x_ref, wg_ref, bg_ref, wu_ref, bu_ref, wd_ref, bd_ref, y_ref,
                wgb_scr, wub_scr, wdb_scr):
    i = pl.program_id(0)
    prev = be_ref[jnp.maximum(i - 1, 0)]

    @pl.when((i == 0) | (be_ref[i] != prev))
    def _():
        wgb_scr[...] = wg_ref[0].astype(BF16)
        wub_scr[...] = wu_ref[0].astype(BF16)
        wdb_scr[...] = wd_ref[0].astype(BF16)

    @pl.when(i < nu_ref[0])
    def _():
        x = x_ref[...].astype(BF16)
        d_ff = wgb_scr.shape[1]
        y = jnp.zeros(y_ref.shape, F32) + bd_ref[0]
        for f0 in range(0, d_ff, _FF_CHUNK):
            f1 = f0 + _FF_CHUNK
            g = jnp.dot(x, wgb_scr[:, f0:f1], preferred_element_type=F32) + bg_ref[0, :, f0:f1]
            u = jnp.dot(x, wub_scr[:, f0:f1], preferred_element_type=F32) + bu_ref[0, :, f0:f1]
            g = jnp.minimum(g, SWIGLU_LIMIT)
            u = jnp.clip(u, -SWIGLU_LIMIT, SWIGLU_LIMIT)
            act = g * (1.0 / (1.0 + jnp.exp(-SWIGLU_ALPHA * g))) * (u + 1.0)
            y = y + jnp.dot(act.astype(BF16), wdb_scr[f0:f1, :], preferred_element_type=F32)
        y_ref[...] = y

    @pl.when(i >= nu_ref[0])
    def _():
        y_ref[...] = jnp.zeros(y_ref.shape, F32)


def _moe_experts(xs, block_e, n_used, wg, bg, wu, bu, wd, bd, tm):
    rows, d = xs.shape
    n_blocks = rows // tm
    d_ff = wg.shape[2]
    wspec = lambda a: pl.BlockSpec((1,) + a.shape[1:], lambda i, be, nu: (be[i], 0, 0))
    grid_spec = pltpu.PrefetchScalarGridSpec(
        num_scalar_prefetch=2,
        grid=(n_blocks,),
        in_specs=[pl.BlockSpec((tm, d), lambda i, be, nu: (jnp.minimum(i, nu[0] - 1), 0)),
                  wspec(wg), wspec(bg), wspec(wu), wspec(bu), wspec(wd), wspec(bd)],
        out_specs=pl.BlockSpec((tm, d), lambda i, be, nu: (i, 0)),
        scratch_shapes=[pltpu.VMEM((d, d_ff), BF16), pltpu.VMEM((d, d_ff), BF16), pltpu.VMEM((d_ff, d), BF16)],
    )
    return pl.pallas_call(
        _moe_kernel,
        grid_spec=grid_spec,
        out_shape=jax.ShapeDtypeStruct((rows, d), F32),
        compiler_params=_cparams(("arbitrary",)),
        name="moe_experts",
    )(block_e, n_used, xs, wg, bg, wu, bu, wd, bd)


def _combine_kernel(nt_a, dcur_ref, dnxt_ref, h1_ref, rt_ref, g_ref, b_ref, yb_hbm, oa_ref, ob_ref, gbuf, sem):
    i = pl.program_id(0)
    tm = h1_ref.shape[0]

    def gather(dref, slot, wait):
        def body(r, c):
            for k in range(TOP_K):
                cp = pltpu.make_async_copy(yb_hbm.at[pl.ds(dref[r * TOP_K + k], 1)],
                                           gbuf.at[slot, k, pl.ds(r, 1)], sem.at[slot])
                if wait:
                    cp.wait()
                else:
                    cp.start()
            return c
        lax.fori_loop(0, tm, body, 0, unroll=4)

    @pl.when(i == 0)
    def _():
        gather(dcur_ref, 0, False)

    @pl.when(i + 1 < pl.num_programs(0))
    def _():
        gather(dnxt_ref, (i + 1) % 2, False)

    slot = i % 2
    gather(dcur_ref, slot, True)
    rt = rt_ref[...]
    f = rt[:, RT_GATE:RT_GATE + 1] * gbuf[slot, 0]
    for k in range(1, TOP_K):
        f = f + rt[:, RT_GATE + k:RT_GATE + k + 1] * gbuf[slot, k]
    y = _layer_norm(DEEPNORM_ALPHA * h1_ref[...] + f, g_ref[...], b_ref[...])

    @pl.when(i < nt_a)
    def _():
        oa_ref[...] = y

    @pl.when(i >= nt_a)
    def _():
        ob_ref[...] = y


def _combine_ln2(h1, rt, dest_flat, yb, g, b, n_a, tm):
    n, d = h1.shape
    nt = n // tm
    nt_a = n_a // tm
    assert n_a % tm == 0 and 0 < nt_a < nt
    tok = lambda w: pl.BlockSpec((tm, w), lambda i: (i, 0))
    full = lambda a: pl.BlockSpec(a.shape, lambda i: (0,) * a.ndim)
    return pl.pallas_call(
        functools.partial(_combine_kernel, nt_a),
        grid=(nt,),
        in_specs=[pl.BlockSpec((tm * TOP_K,), lambda i: (i,), memory_space=pltpu.SMEM),
                  pl.BlockSpec((tm * TOP_K,), lambda i: (jnp.minimum(i + 1, nt - 1),), memory_space=pltpu.SMEM),
                  tok(d), tok(LANES), full(g), full(b), pl.BlockSpec(memory_space=pl.ANY)],
        out_specs=[pl.BlockSpec((tm, d), lambda i: (jnp.minimum(i, nt_a - 1), 0)),
                   pl.BlockSpec((tm, d), lambda i: (jnp.maximum(i - nt_a, 0), 0))],
        out_shape=[jax.ShapeDtypeStruct((n_a, d), F32), jax.ShapeDtypeStruct((n - n_a, d), F32)],
        scratch_shapes=[pltpu.VMEM((2, TOP_K, tm, d), F32), pltpu.SemaphoreType.DMA((2,))],
        compiler_params=_cparams(("arbitrary",)),
        name="moe_combine_ln2",
    )(dest_flat, dest_flat, h1, rt, g, b, yb)


def _layout(rt, counts, n, tm):
    counts = counts.astype(jnp.int32)
    padded = (counts + tm - 1) // tm * tm
    pad_end = jnp.cumsum(padded)
    pad_start = pad_end - padded
    eid = rt[:, RT_EID:RT_EID + TOP_K].astype(jnp.int32)
    pos = rt[:, RT_POS:RT_POS + TOP_K].astype(jnp.int32)
    dest = (pad_start[eid] + pos).reshape(-1)
    n_blocks = -(-(n * TOP_K) // tm) + N_EXPERTS
    first_row = jnp.arange(n_blocks, dtype=jnp.int32) * tm
    block_e = jnp.minimum(jnp.sum(pad_end[None, :] <= first_row[:, None], axis=1), N_EXPERTS - 1).astype(jnp.int32)
    n_used = (pad_end[-1] // tm).astype(jnp.int32).reshape(1)
    return dest, block_e, n_used, (pad_start + counts).astype(jnp.int32), pad_end.astype(jnp.int32), n_blocks * tm


def _hd_rows(a):
    nd = a.ndim
    t = jnp.transpose(a, tuple(range(nd - 3)) + (nd - 2, nd - 1, nd - 3))
    return t.reshape(t.shape[:-3] + (t.shape[-3] * t.shape[-2], t.shape[-1]))


def _rows_hd(at, heads):
    t = at.reshape(at.shape[:-2] + (heads, at.shape[-2] // heads, at.shape[-1]))
    nd = t.ndim
    return jnp.transpose(t, tuple(range(nd - 3)) + (nd - 1, nd - 3, nd - 2))


def kernel(x_prompt, x_sample, cache_k, cache_v, page_table, state_gla, cache_mem_k, cache_mem_v, mem_prompt,
           ln_in_g, ln_in_b, w_in, w_gla_gate, b_gla_gate, g_gla, w_mem_kv, w_out, ln1_g, ln1_b,
           w_router, b_router, w_gate, b_gate, w_up, b_up, w_down, b_down, ln2_g, ln2_b):
    assert w_in.shape[0] == DEPTH == 1
    bz, seq, d = x_prompt.shape
    db, t_new, _ = x_sample.shape
    page = cache_k.shape[2]
    past = page_table.shape[1] * page
    mem_len = mem_prompt.shape[1]
    n_p, n_s = bz * seq, db * t_new
    n = n_p + n_s
    tm = TOK_TM
    t_pad = 8
    row = lambda a: a.reshape(1, -1)

    wi = w_in[0]
    c_lg = 3 * A_W + 2 * G_K + G_V
    w_perm = jnp.concatenate([wi[:, :c_lg], wi[:, c_lg + G_LOWRANK:], wi[:, c_lg:c_lg + G_LOWRANK],
                              jnp.zeros((d, LANES - G_LOWRANK), F32)], axis=1).astype(BF16)
    wgg = jnp.zeros((LANES, G_K), F32).at[:G_LOWRANK].set(w_gla_gate[0])
    bgg = row(b_gla_gate[0])
    gg = row(jnp.tile(g_gla[0], G_HEADS))
    wo = w_out[0].astype(BF16)
    wr = jnp.zeros((d, LANES), F32).at[:, :N_EXPERTS].set(w_router[0])
    br = jnp.full((1, LANES), -jnp.inf, F32).at[0, :N_EXPERTS].set(b_router[0])
    ln_g, ln_b = row(ln_in_g), row(ln_in_b)

    tabs_p = _rope_tables(np.arange(seq))
    tabs_s = _rope_tables(np.tile(past + np.arange(t_new), db))
    qp, ktp, vtp, gqp, gkp, gvp, lap, rgp, qmp = _project(x_prompt, tabs_p, ln_g, ln_b, w_perm, wgg, bgg, tm)
    xs3 = x_sample.reshape(1, n_s, d)
    qs, kts, vts, gqs, gks, gvs, las, rgs, qms = _project(xs3, tabs_s, ln_g, ln_b, w_perm, wgg, bgg, n_s)
    ks, vs = kts[0].T, vts[0].T
    mkt_p, mvt_p = _mem_kv(mem_prompt, w_mem_kv[0].astype(BF16))

    oa_p = _moba_prompt(qp, ktp, vtp)
    s0_p = jnp.zeros((bz, G_V, G_K), F32)
    og_p, st_p = _gla(gqp, gkp, lap, gvp, rgp, s0_p, gg, GLA_CHUNK, MOBA_BLOCK // GLA_CHUNK, True)
    om_p = _mem_attn(qmp, mkt_p, mvt_p, tm)

    pad_t = lambda a: jnp.pad(a.reshape(db, t_new, -1), ((0, 0), (0, t_pad - t_new), (0, 0)))
    oa_s = _moba_sample(pad_t(qs), pad_t(ks), pad_t(vs), _hd_rows(cache_k), _hd_rows(cache_v), page_table,
                        t_new)[:, :t_new]
    seq_per_step = 16
    grp = lambda a: pad_t(a).reshape(db // seq_per_step, seq_per_step * t_pad, -1)
    og_s, st_s = _gla(grp(gqs), grp(gks), grp(las), grp(gvs), grp(rgs), _state_to_t(state_gla[0]), gg,
                      t_pad, seq_per_step, False)
    og_s = og_s.reshape(db, t_pad, G_V)[:, :t_new]
    om_s = _mem_attn(pad_t(qms), _hd_rows(cache_mem_k[0]), _hd_rows(cache_mem_v[0]), t_pad)[:, :t_new]

    mix_w = (ln_g, ln_b, wo, row(ln1_g[0]), row(ln1_b[0]), wr, br)
    flat = lambda a, rows: a.reshape(rows, -1)
    h1, rt, cnt = _mix_out(flat(x_prompt, n_p), flat(oa_p, n_p), flat(og_p, n_p), flat(om_p, n_p), *mix_w,
                           jnp.zeros((1, LANES), F32), jnp.zeros((n, d), F32), jnp.zeros((n, LANES), F32), 0, tm)
    h1, rt, cnt = _mix_out(flat(x_sample, n_s), flat(oa_s, n_s), flat(og_s, n_s), flat(om_s, n_s), *mix_w,
                           cnt, h1, rt, n_p, tm)

    dest, block_e, n_used, z_start, z_end, n_rows = _layout(rt, cnt[0, :N_EXPERTS], n, tm)
    xs_sorted = _dispatch(h1, dest, z_start, z_end, n_rows, tm)
    b3 = lambda a: a[0].reshape(N_EXPERTS, 1, -1)
    yb = _moe_experts(xs_sorted, block_e, n_used, w_gate[0], b3(b_gate), w_up[0], b3(b_up), w_down[0], b3(b_down), tm)
    y_p, y_s = _combine_ln2(h1, rt, dest, yb, row(ln2_g[0]), row(ln2_b[0]), n_p, tm)

    return (y_p.reshape(bz, seq, d), y_s.reshape(db, t_new, d),
            _rows_hd(ktp, A_HEADS)[None], _rows_hd(vtp, A_HEADS)[None],
            _state_from_t(st_p)[None],
            _rows_hd(mkt_p, M_HEADS)[None], _rows_hd(mvt_p, M_HEADS)[None],
            ks.reshape(1, db, t_new, A_HEADS, A_DIM), vs.reshape(1, db, t_new, A_HEADS, A_DIM),
            _state_from_t(st_s)[None])
```

```python
import functools

import jax
import jax.numpy as jnp
import numpy as np
from jax import lax
from jax.experimental import pallas as pl
from jax.experimental.pallas import tpu as pltpu

F32 = jnp.float32
BF16 = jnp.bfloat16
HI = lax.Precision.HIGHEST
NT = (((1,), (1,)), ((), ()))

A_HEADS, A_DIM = 8, 64
ROT_DIM = A_DIM // 4
ROPE_THETA = 500000.0
MOBA_BLOCK, MOBA_TOPK = 256, 3
G_HEADS, G_DK, G_DV, G_LOWRANK, G_TAU = 4, 32, 64, 16, 16.0
M_HEADS, M_DIM = 4, 64
A_W = A_HEADS * A_DIM
G_K = G_HEADS * G_DK
G_V = G_HEADS * G_DV
M_W = M_HEADS * M_DIM
N_EXPERTS, TOP_K = 32, 4
SWIGLU_ALPHA, SWIGLU_LIMIT = 1.702, 7.0
LN_EPS, RMS_EPS = 1e-5, 1e-6
NEG = -1e30
LOG2E = 1.4426950408889634
DEPTH = 1
DEEPNORM_ALPHA = (2 * DEPTH) ** 0.25

LANES = 128
VMEM_LIMIT = 56 * 1024 * 1024

TOK_TM = 512
GLA_CHUNK = 16
RT_GATE, RT_EID, RT_POS = 0, TOP_K, 2 * TOP_K


def _cparams(sem):
    return pltpu.CompilerParams(dimension_semantics=sem, vmem_limit_bytes=VMEM_LIMIT)


def _hi_lo(x):
    hi = x.astype(BF16)
    return hi, (x - hi.astype(F32)).astype(BF16)


def _dot_exact_lhs(m, x):
    hi, lo = _hi_lo(x)
    return jnp.dot(m, hi, preferred_element_type=F32) + jnp.dot(m, lo, preferred_element_type=F32)


def _layer_norm(x, g, b):
    mu = jnp.mean(x, axis=-1, keepdims=True)
    xc = x - mu
    var = jnp.mean(xc * xc, axis=-1, keepdims=True)
    return xc * lax.rsqrt(var + LN_EPS) * g + b


_C_QA, _C_KA, _C_VA = 0, A_W, 2 * A_W
_C_QG = 3 * A_W
_C_KG = _C_QG + G_K
_C_VG = _C_KG + G_K
_C_RG = _C_VG + G_V
_C_QM = _C_RG + G_V
_C_LG = _C_QM + M_W
_C_END = _C_LG + LANES


def _proj_kernel(x_ref, g_ref, b_ref, w_ref, wgg_ref, bgg_ref, c_ref, s1_ref, s2_ref,
                 q_ref, kt_ref, vt_ref, gq_ref, gk_ref, gv_ref, la_ref, rg_ref, qm_ref):
    h = _layer_norm(x_ref[0], g_ref[...], b_ref[...]).astype(BF16)

    def mm(lo, hi):
        return jnp.dot(h, w_ref[:, lo:hi], preferred_element_type=F32)

    c, s1, s2 = (jnp.tile(t[...], (1, A_W // LANES)) for t in (c_ref, s1_ref, s2_ref))
    half = ROT_DIM // 2

    def rope(t):
        return t * c + pltpu.roll(t, A_W - half, 1) * s1 + pltpu.roll(t, half, 1) * s2

    q_ref[0] = rope(mm(_C_QA, _C_KA)) * (A_DIM ** -0.5 * LOG2E)
    kt_ref[0] = rope(mm(_C_KA, _C_VA)).T
    vt_ref[0] = mm(_C_VA, _C_QG).T
    gq_ref[0] = mm(_C_QG, _C_KG)
    gk_ref[0] = mm(_C_KG, _C_VG)
    gv_ref[0] = mm(_C_VG, _C_RG)
    rg_ref[0] = mm(_C_RG, _C_QM)
    qm_ref[0] = mm(_C_QM, _C_LG) * (M_DIM ** -0.5)
    lg_hi, lg_lo = _hi_lo(mm(_C_LG, _C_END))
    d3 = lambda a, w: jnp.dot(a, w, preferred_element_type=F32)
    z = d3(lg_hi, wgg_ref[0]) + (d3(lg_lo, wgg_ref[0]) + d3(lg_hi, wgg_ref[1])) + bgg_ref[...]
    la_ref[0] = (jnp.minimum(z, 0.0) - jnp.log(1.0 + jnp.exp(-jnp.abs(z)))) * (1.0 / G_TAU)


def _rope_tables(pos):
    half = ROT_DIM // 2
    f32 = np.float32
    inv = np.power(f32(ROPE_THETA), -np.arange(half, dtype=f32) / f32(half)).astype(f32)
    ang = pos.astype(f32)[:, None] * inv[None, :]
    cos, sin = np.cos(ang).astype(f32), np.sin(ang).astype(f32)
    n = pos.shape[0]
    one = np.ones((n, A_DIM - ROT_DIM), f32)
    zero8 = np.zeros((n, half), f32)
    zero = np.zeros((n, A_DIM - ROT_DIM), f32)
    c = np.concatenate([cos, cos, one], axis=1)
    s1 = np.concatenate([-sin, zero8, zero], axis=1)
    s2 = np.concatenate([zero8, sin, zero], axis=1)
    return tuple(jnp.asarray(np.tile(t, (1, LANES // A_DIM))) for t in (c, s1, s2))


def _project(x, pos_tables, ln_g, ln_b, w_perm, wgg, bgg, ts):
    bz, s, d = x.shape
    widths = (A_W, None, None, G_K, G_K, G_V, G_K, G_V, M_W)
    tok = lambda w: (pl.BlockSpec((1, ts, w), lambda j, b: (b, j, 0)) if w else
                     pl.BlockSpec((1, A_W, ts), lambda j, b: (b, 0, j)))
    full = lambda a: pl.BlockSpec(a.shape, lambda j, b: (0,) * a.ndim)
    tab = pl.BlockSpec((ts, LANES), lambda j, b: (j, 0))
    return pl.pallas_call(
        _proj_kernel,
        grid=(s // ts, bz),
        in_specs=[tok(d), full(ln_g), full(ln_b), full(w_perm), full(wgg), full(bgg), tab, tab, tab],
        out_specs=[tok(w) for w in widths],
        out_shape=[jax.ShapeDtypeStruct((bz, s, w) if w else (bz, A_W, s), F32) for w in widths],
        compiler_params=_cparams(("arbitrary", "arbitrary")),
        name="proj",
    )(x, ln_g, ln_b, w_perm, wgg, bgg, *pos_tables)


def _mem_kv_kernel(x_ref, w_ref, kt_ref, vt_ref):
    kv = jnp.dot(x_ref[0].astype(BF16), w_ref[...], preferred_element_type=F32)
    kt_ref[0] = kv[:, :M_W].T
    vt_ref[0] = kv[:, M_W:].T


def _mem_kv(mem, w_bf16):
    bz, m, d = mem.shape
    out = pl.BlockSpec((1, M_W, m), lambda b: (b, 0, 0))
    return pl.pallas_call(
        _mem_kv_kernel,
        grid=(bz,),
        in_specs=[pl.BlockSpec((1, m, d), lambda b: (b, 0, 0)), pl.BlockSpec(w_bf16.shape, lambda b: (0, 0))],
        out_specs=[out, out],
        out_shape=[jax.ShapeDtypeStruct((bz, M_W, m), F32)] * 2,
        compiler_params=_cparams(("arbitrary",)),
        name="mem_kv",
    )(mem, w_bf16)


_MOBA_HG = 4


_MOBA_VA = A_DIM + 16


def _moba_prompt_kernel(nblk, q_ref, kt_ref, vt_ref, o_ref, kb_scr, vt_scr, km_scr, selb_scr, acc_scr,
                        sa_scr, sb_scr, sd_scr):
    blk = MOBA_BLOCK
    hg = _MOBA_HG
    va = _MOBA_VA
    i = pl.program_id(2)

    @pl.when(i == 0)
    def _():
        lane = lax.broadcasted_iota(jnp.int32, (1, hg * A_DIM), 1)
        ones = jnp.ones((va - A_DIM, blk), BF16)
        for j in range(nblk):
            kj = kt_ref[0, :, j * blk:(j + 1) * blk].T
            kb_scr[j] = kj.astype(BF16)
            kmj = jnp.mean(kj, axis=0, keepdims=True)
            vj = vt_ref[0, :, j * blk:(j + 1) * blk].astype(BF16)
            for h in range(hg):
                km_scr[h * nblk + j:h * nblk + j + 1, :] = jnp.where(lane // A_DIM == h, kmj, 0.0)
                vt_scr[j, h * va:h * va + A_DIM, :] = vj[h * A_DIM:(h + 1) * A_DIM, :]
                vt_scr[j, h * va + A_DIM:(h + 1) * va, :] = ones

    q_t = q_ref[0].T
    row = lax.broadcasted_iota(jnp.int32, q_t.shape, 0)
    q_tb = [jnp.where((row >= A_DIM * h) & (row < A_DIM * (h + 1)), q_t, 0.0).astype(BF16) for h in range(hg)]
    km_hi, km_lo = _hi_lo(km_scr[...])
    q_hi, q_lo = _hi_lo(q_t)
    d3 = lambda a, w: jnp.dot(a, w, preferred_element_type=F32)
    g = (d3(km_hi, q_hi) + (d3(km_lo, q_hi) + d3(km_hi, q_lo))).reshape(hg, nblk, blk)
    blk_iota = lax.broadcasted_iota(jnp.int32, g.shape, 1)
    cnt = jnp.zeros(g.shape, jnp.int32)
    for jp in range(nblk):
        gj = g[:, jp:jp + 1, :]
        beats = (gj > g) | ((gj == g) & (jp < blk_iota))
        cnt = cnt + jnp.where(beats, 1, 0) * (jp < i).astype(jnp.int32)
    selb_scr[...] = jnp.where((blk_iota < i) & (cnt < MOBA_TOPK), 0.0, NEG)

    def scores(j, s_ref):
        kj = kb_scr[j]
        for h in range(hg):
            s_ref[h] = jnp.dot(kj, q_tb[h], preferred_element_type=F32)

    kpos = lax.broadcasted_iota(jnp.int32, (blk, blk), 0)
    qpos = lax.broadcasted_iota(jnp.int32, (blk, blk), 1)

    def absorb(j, s_ref, ms):
        vtj = vt_scr[j]
        new = []
        for h in range(hg):
            s = s_ref[h]
            if ms is None:
                s = jnp.where(kpos <= qpos, s, NEG)
                m_new = shift = jnp.max(s, axis=0, keepdims=True)
            else:
                bias = selb_scr[h, pl.ds(j, 1), :]
                m_new = jnp.maximum(ms[h], jnp.max(s, axis=0, keepdims=True) + bias)
                shift = m_new - bias
            pv = jnp.dot(vtj[h * va:(h + 1) * va, :], jnp.exp2(s - shift).astype(BF16), preferred_element_type=F32)
            acc_scr[h] = pv if ms is None else jnp.exp2(ms[h] - m_new) * acc_scr[h] + pv
            new.append(m_new)
        return tuple(new)

    last = nblk - 1
    scores(i, sd_scr)
    scores(0, sa_scr)
    ms = absorb(i, sd_scr, None)

    def past(jj, ms):
        ja, jb = 2 * jj, 2 * jj + 1
        scores(jb, sb_scr)
        ms = absorb(ja, sa_scr, ms)
        scores(jnp.minimum(ja + 2, last), sa_scr)
        return absorb(jb, sb_scr, ms)

    lax.fori_loop(0, (i + 1) // 2, past, ms)
    o_t = jnp.concatenate([acc_scr[h, :A_DIM, :] / acc_scr[h, A_DIM:A_DIM + 1, :] for h in range(hg)], axis=0)
    o_ref[0] = o_t.T


def _moba_prompt(q, kt, vt):
    bz, s, _ = q.shape
    blk = MOBA_BLOCK
    nblk = s // blk
    assert nblk % 2 == 0
    w = _MOBA_HG * A_DIM
    return pl.pallas_call(
        functools.partial(_moba_prompt_kernel, nblk),
        grid=(bz, A_W // w, nblk),
        in_specs=[pl.BlockSpec((1, blk, w), lambda b, hp, i: (b, i, hp)),
                  pl.BlockSpec((1, w, s), lambda b, hp, i: (b, hp, 0)),
                  pl.BlockSpec((1, w, s), lambda b, hp, i: (b, hp, 0))],
        out_specs=pl.BlockSpec((1, blk, w), lambda b, hp, i: (b, i, hp)),
        out_shape=jax.ShapeDtypeStruct((bz, s, A_W), F32),
        scratch_shapes=[pltpu.VMEM((nblk, blk, w), BF16), pltpu.VMEM((nblk, _MOBA_HG * _MOBA_VA, blk), BF16),
                        pltpu.VMEM((_MOBA_HG * nblk, w), F32), pltpu.VMEM((_MOBA_HG, nblk, blk), F32),
                        pltpu.VMEM((_MOBA_HG, _MOBA_VA, blk), F32)]
                       + [pltpu.VMEM((_MOBA_HG, blk, blk), F32)] * 3,
        compiler_params=_cparams(("arbitrary", "arbitrary", "arbitrary")),
        name="moba_prompt",
    )(q, kt, vt)


_PG_CHUNK = 8
_K_SLOTS = 4


def _moba_sample_kernel(t_new, n_pages, page, pt_ref, q_ref, kn_ref, vn_ref, ck_hbm, cv_hbm, o_ref,
                        kbuf, vbuf, ksem, vsem, fsem, s_scr, p_scr, bias_scr, flag_v, flag_s, unit_s):
    b = pl.program_id(0)
    nb = pl.num_programs(0)
    n_chunks = n_pages // _PG_CHUNK
    ppb = MOBA_BLOCK // page
    n_past = n_pages // ppb
    rows = t_new * A_HEADS

    def k_copy(bb, u, pg):
        slot = u % _K_SLOTS
        return pltpu.make_async_copy(ck_hbm.at[0, pt_ref[bb, u * _PG_CHUNK + pg]], kbuf.at[slot, pg], ksem.at[slot])

    def k_start(bb, u):
        for pg in range(_PG_CHUNK):
            k_copy(bb, u, pg).start()

    @pl.when(b == 0)
    def _():
        for u in range(_K_SLOTS):
            k_start(b, u)

        def zero(pg, c):
            vbuf[pg] = jnp.zeros(vbuf.shape[1:], F32)
            return c
        lax.fori_loop(0, n_pages, zero, 0)

    lane = lax.broadcasted_iota(jnp.int32, (A_HEADS, A_W), 1)
    sub = lax.broadcasted_iota(jnp.int32, (A_HEADS, A_W), 0)
    headmask = (lane // A_DIM == sub).astype(F32)
    q = q_ref[0]
    qbd = jnp.concatenate([jnp.broadcast_to(q[t:t + 1, :], (A_HEADS, A_W)) * headmask for t in range(t_new)],
                          axis=0)
    q_hi = qbd.astype(BF16)
    q_lo = (qbd - q_hi.astype(F32)).astype(BF16)
    q_hl = jnp.concatenate([q_hi, q_lo], axis=0)

    for u in range(n_chunks):
        for pg in range(_PG_CHUNK):
            k_copy(b, u, pg).wait()
        for pg in range(_PG_CHUNK):
            s2 = jnp.dot(q_hl, kbuf[u % _K_SLOTS, pg].astype(BF16), preferred_element_type=F32)
            s_scr[u * _PG_CHUNK + pg] = s2[:rows] + s2[rows:]
        if u + _K_SLOTS < n_chunks:
            k_start(b, u + _K_SLOTS)

    def next_k(lo, hi):
        @pl.when(b + 1 < nb)
        def _():
            for u in range(lo, hi):
                k_start(b + 1, u)
    next_k(0, _K_SLOTS // 2)

    col = lax.broadcasted_iota(jnp.int32, (rows, n_past), 1)
    g = jnp.zeros((rows, n_past), F32)
    for jb in range(n_past):
        blk_s = s_scr[jb * ppb]
        for e in range(1, ppb):
            blk_s = blk_s + s_scr[jb * ppb + e]
        g = jnp.where(col == jb, jnp.sum(blk_s, axis=1, keepdims=True), g)
    cnt = jnp.zeros((rows, n_past), jnp.int32)
    for jp in range(n_past):
        gj = g[:, jp:jp + 1]
        cnt = cnt + jnp.where((gj > g) | ((gj == g) & (jp < col)), 1, 0)
    sel = cnt < MOBA_TOPK

    picked = jnp.where(sel, 1, 0)
    head_any = picked[0:A_HEADS]
    for t in range(1, t_new):
        head_any = jnp.maximum(head_any, picked[t * A_HEADS:(t + 1) * A_HEADS])
    flag_v[...] = jnp.zeros(flag_v.shape, jnp.int32)
    flag_v[:, 0:n_past] = head_any
    flag_copy = pltpu.make_async_copy(flag_v, flag_s, fsem)
    flag_copy.start()
    flag_copy.wait()

    def collect(jb, n_units):
        for h in range(A_HEADS):
            unit_s[n_units] = jb * A_HEADS + h
            n_units = n_units + flag_s[h, jb]
        return n_units
    n_units = lax.fori_loop(0, n_past, collect, 0)

    def v_copies(idx):
        unit = unit_s[idx]
        jb = unit // A_HEADS
        r0 = pl.multiple_of((unit % A_HEADS) * A_DIM, A_DIM)
        return [pltpu.make_async_copy(cv_hbm.at[0, pt_ref[b, jb * ppb + e], pl.ds(r0, A_DIM), :],
                                      vbuf.at[jb * ppb + e, pl.ds(r0, A_DIM), :], vsem) for e in range(ppb)]

    def v_start(idx, c):
        for cp in v_copies(idx):
            cp.start()
        return c
    lax.fori_loop(0, n_units, v_start, 0)

    next_k(_K_SLOTS // 2, _K_SLOTS)

    selb = jnp.where(sel, 0.0, NEG)
    for jb in range(n_past):
        bias_scr[jb] = jnp.broadcast_to(selb[:, jb:jb + 1], (rows, page))
    kn, vn = kn_ref[0], vn_ref[0]
    r_t = lax.broadcasted_iota(jnp.int32, (rows, 1), 0) // A_HEADS
    s_own = [jnp.where(r_t >= t, jnp.sum(qbd * kn[t:t + 1, :], axis=1, keepdims=True), NEG) for t in range(t_new)]

    def smax(pg, mx):
        sb = s_scr[pg] + bias_scr[pg // ppb]
        s_scr[pg] = sb
        return jnp.maximum(mx, sb)
    mx = lax.fori_loop(0, n_pages, smax, jnp.full((rows, page), NEG, F32), unroll=4)
    m = jnp.max(mx, axis=1, keepdims=True)
    for t in range(t_new):
        m = jnp.maximum(m, s_own[t])

    def sexp(pg, ls):
        p = jnp.exp2(s_scr[pg] - m)
        p_scr[pg] = p.astype(BF16)
        return ls + p
    ls = lax.fori_loop(0, n_pages, sexp, jnp.zeros((rows, page), F32), unroll=4)
    l = jnp.sum(ls, axis=1, keepdims=True)
    acc = jnp.zeros((rows, A_W), F32)
    for t in range(t_new):
        p_t = jnp.exp2(s_own[t] - m)
        l = l + p_t
        acc = acc + p_t * vn[t:t + 1, :]

    def v_wait(idx, c):
        for cp in v_copies(idx):
            cp.wait()
        return c
    lax.fori_loop(0, n_units, v_wait, 0)

    def v_page(pg, a):
        return a + lax.dot_general(p_scr[pg], vbuf[pg].astype(BF16), NT, preferred_element_type=F32)
    acc = lax.fori_loop(0, n_pages, v_page, acc, unroll=8)

    o = acc / l
    out_rows = [jnp.sum(o[t * A_HEADS:(t + 1) * A_HEADS, :] * headmask, axis=0, keepdims=True)
                for t in range(t_new)]
    out_rows.append(jnp.zeros((o_ref.shape[1] - t_new, A_W), F32))
    o_ref[0] = jnp.concatenate(out_rows, axis=0)


def _moba_sample(q8, kn8, vn8, ck, cv, page_table, t_new):
    db, tp, _ = q8.shape
    n_pages = page_table.shape[1]
    page = ck.shape[3]
    assert (n_pages * page) % MOBA_BLOCK == 0, "past length must end on a MoBA block boundary"
    assert MOBA_BLOCK % page == 0 and n_pages % (_PG_CHUNK * _K_SLOTS) == 0
    n_past = n_pages * page // MOBA_BLOCK
    assert MOBA_TOPK <= n_past <= LANES
    rows = t_new * A_HEADS
    tok = pl.BlockSpec((1, tp, A_W), lambda b, pt: (b, 0, 0))
    any_spec = pl.BlockSpec(memory_space=pl.ANY)
    grid_spec = pltpu.PrefetchScalarGridSpec(
        num_scalar_prefetch=1,
        grid=(db,),
        in_specs=[tok, tok, tok, any_spec, any_spec],
        out_specs=tok,
        scratch_shapes=[pltpu.VMEM((_K_SLOTS, _PG_CHUNK, A_W, page), F32), pltpu.VMEM((n_pages, A_W, page), F32),
                        pltpu.SemaphoreType.DMA((_K_SLOTS,)), pltpu.SemaphoreType.DMA(()), pltpu.SemaphoreType.DMA(()),
                        pltpu.VMEM((n_pages, rows, page), F32), pltpu.VMEM((n_pages, rows, page), BF16),
                        pltpu.VMEM((n_past, rows, page), F32),
                        pltpu.VMEM((A_HEADS, LANES), jnp.int32), pltpu.SMEM((A_HEADS, LANES), jnp.int32),
                        pltpu.SMEM((n_past * A_HEADS,), jnp.int32)],
    )
    return pl.pallas_call(
        functools.partial(_moba_sample_kernel, t_new, n_pages, page),
        grid_spec=grid_spec,
        out_shape=jax.ShapeDtypeStruct((db, tp, A_W), F32),
        compiler_params=_cparams(("arbitrary",)),
        name="moba_sample",
    )(page_table, q8, kn8, vn8, ck, cv)


def _gla_kernel(c, nch, carry, q_ref, k_ref, la_ref, v_ref, rg_ref, s0_ref, gg_ref, l_ref, e_ref, hm_ref,
                bd_ref, hv_ref, o_ref, sf_ref, st_scr, b_scr, qs_scr, k_scr, qb_scr, dch_scr, oo_scr):
    j = pl.program_id(1)
    if carry:
        @pl.when(j == 0)
        def _():
            st_scr[...] = s0_ref[0]

    la = la_ref[0]
    b = _dot_exact_lhs(l_ref[...], la)
    tot = _dot_exact_lhs(e_ref[...], la)
    qs = q_ref[0] * (G_DK ** -0.5)
    k = k_ref[0]
    b_scr[...] = b
    qs_scr[...] = qs
    k_scr[...] = k
    qb_scr[...] = qs * jnp.exp(b)
    dch_scr[...] = jnp.exp(tot)
    kdec = (k * jnp.exp(tot - b)).astype(BF16)
    v_t = v_ref[0].T
    lane_chunk = lax.broadcasted_iota(jnp.int32, v_t.shape, 1) // c
    jrow = lax.broadcasted_iota(jnp.int32, (c, G_K), 0)

    def chunk(ci, st):
        r0 = pl.multiple_of(ci * c, c)
        if not carry:
            st = s0_ref[ci]
        o_inter = lax.dot_general(qb_scr[pl.ds(r0, c), :].astype(BF16), st.astype(BF16), NT,
                                  preferred_element_type=F32)
        b_c, k_c, qs_c = b_scr[pl.ds(r0, c), :], k_scr[pl.ds(r0, c), :], qs_scr[pl.ds(r0, c), :]
        terms = []
        for i in range(c):
            d = jnp.where(jrow <= i, b_c[i:i + 1, :] - b_c, NEG)
            terms.append(qs_c[i:i + 1, :] * k_c * jnp.exp(d))
        t_all = jnp.concatenate(terms, axis=0).astype(BF16)
        att = jnp.dot(t_all, hm_ref[...], preferred_element_type=F32)
        v_c = v_ref[0, pl.ds(r0, c), :]
        o_intra = jnp.sum(att.reshape(c, c, G_V) * v_c[None, :, :], axis=1)
        oo_scr[pl.ds(r0, c), :] = o_inter + o_intra
        v_m = jnp.where(lane_chunk == ci, v_t, 0.0).astype(BF16)
        upd = jnp.dot(v_m, kdec, preferred_element_type=F32) * bd_ref[...]
        st_new = st * dch_scr[pl.ds(r0, 1), :] + upd
        if not carry:
            sf_ref[ci] = st_new
        return st_new

    st = lax.fori_loop(0, nch, chunk, st_scr[...] if carry else jnp.zeros(st_scr.shape, F32))
    if carry:
        st_scr[...] = st

        @pl.when(j == pl.num_programs(1) - 1)
        def _():
            sf_ref[0] = st

    o = oo_scr[...]
    sq_hi, sq_lo = _hi_lo(o * o)
    hv = hv_ref[...]
    ms = jnp.dot(sq_hi, hv, preferred_element_type=F32) + jnp.dot(sq_lo, hv, preferred_element_type=F32)
    rg = rg_ref[0]
    o_ref[0] = o * lax.rsqrt(ms + RMS_EPS) * gg_ref[...] * (rg / (1.0 + jnp.exp(-rg)))


def _gla(gq, gk, la, gv, rg, s0_t, gg, c, nch, carry):
    bz, t, _ = gq.shape
    tt = c * nch
    idx = jnp.arange(tt)
    same = (idx[:, None] // c) == (idx[None, :] // c)
    l_mat = (same & (idx[None, :] <= idx[:, None])).astype(BF16)
    e_mat = same.astype(BF16)
    hm = ((jnp.arange(G_K)[:, None] // G_DK) == (jnp.arange(G_V)[None, :] // G_DV)).astype(BF16)
    bd = ((jnp.arange(G_V)[:, None] // G_DV) == (jnp.arange(G_K)[None, :] // G_DK)).astype(F32)
    hv = (((jnp.arange(G_V)[:, None] // G_DV) == (jnp.arange(G_V)[None, :] // G_DV)).astype(F32) / G_DV).astype(BF16)
    assert G_DV & (G_DV - 1) == 0, "1 / G_DV must be exact in bf16"
    tok = lambda w: pl.BlockSpec((1, tt, w), lambda b, j: (b, j, 0))
    full = lambda a: pl.BlockSpec(a.shape, lambda b, j: (0,) * a.ndim)
    ns = 1 if carry else nch
    st_spec = pl.BlockSpec((ns, G_V, G_K), lambda b, j: (b, 0, 0))
    return pl.pallas_call(
        functools.partial(_gla_kernel, c, nch, carry),
        grid=(bz, t // tt),
        in_specs=[tok(G_K), tok(G_K), tok(G_K), tok(G_V), tok(G_V), st_spec, full(gg), full(l_mat), full(e_mat),
                  full(hm), full(bd), full(hv)],
        out_specs=[tok(G_V), st_spec],
        out_shape=[jax.ShapeDtypeStruct((bz, t, G_V), F32), jax.ShapeDtypeStruct(s0_t.shape, F32)],
        scratch_shapes=[pltpu.VMEM((G_V, G_K), F32)] + [pltpu.VMEM((tt, G_K), F32)] * 5
                       + [pltpu.VMEM((tt, G_V), F32)],
        compiler_params=_cparams(("arbitrary", "arbitrary")),
        name="gla",
    )(gq, gk, la, gv, rg, s0_t, gg, l_mat, e_mat, hm, bd, hv)


def _state_to_t(s):
    n = s.shape[0]
    eye = jnp.eye(G_HEADS, dtype=s.dtype)
    return jnp.einsum("nhdv,hg->nhvgd", s, eye).reshape(n, G_V, G_K)


def _state_from_t(st):
    n = st.shape[0]
    s5 = st.reshape(n, G_HEADS, G_DV, G_HEADS, G_DK)
    return jnp.stack([s5[:, h, :, h, :] for h in range(G_HEADS)], axis=1).transpose(0, 1, 3, 2)


def _mem_attn_kernel(q_ref, mkt_ref, mvt_ref, o_ref):
    q = q_ref[0]
    mkt = mkt_ref[0].astype(BF16)
    mvt = mvt_ref[0].astype(BF16)
    lane = lax.broadcasted_iota(jnp.int32, q.shape, 1)
    out = jnp.zeros(q.shape, F32)
    for h in range(M_HEADS):
        hsel = (lane // M_DIM) == h
        s = jnp.dot(jnp.where(hsel, q, 0.0).astype(BF16), mkt, preferred_element_type=F32)
        m = jnp.max(s, axis=1, keepdims=True)
        p = jnp.exp(s - m)
        l = jnp.sum(p, axis=1, keepdims=True)
        o_h = lax.dot_general(p.astype(BF16), mvt, NT, preferred_element_type=F32)
        out = out + jnp.where(hsel, o_h / l, 0.0)
    o_ref[0] = out


def _mem_attn(qm, mkt, mvt, tm):
    bz, t, _ = qm.shape
    mem = mkt.shape[2]
    return pl.pallas_call(
        _mem_attn_kernel,
        grid=(bz, t // tm),
        in_specs=[pl.BlockSpec((1, tm, M_W), lambda b, j: (b, j, 0)),
                  pl.BlockSpec((1, M_W, mem), lambda b, j: (b, 0, 0)),
                  pl.BlockSpec((1, M_W, mem), lambda b, j: (b, 0, 0))],
        out_specs=pl.BlockSpec((1, tm, M_W), lambda b, j: (b, j, 0)),
        out_shape=jax.ShapeDtypeStruct((bz, t, M_W), F32),
        compiler_params=_cparams(("arbitrary", "arbitrary")),
        name="mem_attn",
    )(qm, mkt, mvt)


def _mix_out_kernel(x_ref, oa_ref, og_ref, om_ref, lng_ref, lnb_ref, wo_ref, g1_ref, b1_ref, wr_ref, br_ref,
                    tri_ref, cnt0_ref, h1_all, rt_all, h1_ref, rt_ref, cnt_ref, carry_scr):
    del h1_all, rt_all

    @pl.when(pl.program_id(0) == 0)
    def _():
        carry_scr[...] = cnt0_ref[...]

    h = _layer_norm(x_ref[...], lng_ref[...], lnb_ref[...])
    o = jnp.dot(oa_ref[...].astype(BF16), wo_ref[0:A_W, :], preferred_element_type=F32)
    o = o + jnp.dot(og_ref[...].astype(BF16), wo_ref[A_W:A_W + G_V, :], preferred_element_type=F32)
    o = o + jnp.dot(om_ref[...].astype(BF16), wo_ref[A_W + G_V:, :], preferred_element_type=F32)
    h1 = _layer_norm(DEEPNORM_ALPHA * h + o, g1_ref[...], b1_ref[...])
    h1_ref[...] = h1
    h1_hi = h1.astype(BF16)
    h1_lo = (h1 - h1_hi.astype(F32)).astype(BF16)
    mm = lambda a, w: jnp.dot(a, w, preferred_element_type=F32)
    work = mm(h1_hi, wr_ref[0]) + (mm(h1_lo, wr_ref[0]) + mm(h1_hi, wr_ref[1])) + br_ref[...]
    lane = lax.broadcasted_iota(jnp.int32, work.shape, 1).astype(F32)
    onehot = jnp.zeros(work.shape, F32)
    vals, eids = [], []
    for _ in range(TOP_K):
        mk = jnp.max(work, axis=1, keepdims=True)
        ek = jnp.min(jnp.where(work == mk, lane, float(LANES)), axis=1, keepdims=True)
        hit = lane == ek
        onehot = jnp.where(hit, 1.0, onehot)
        work = jnp.where(hit, -jnp.inf, work)
        vals.append(mk)
        eids.append(ek)
    ex = [jnp.exp(v - vals[0]) for v in vals]
    den = ex[0] + ex[1] + ex[2] + ex[3]
    pos_full = jnp.dot(tri_ref[...], onehot.astype(BF16), preferred_element_type=F32) + carry_scr[...]
    carry_scr[...] = carry_scr[...] + jnp.sum(onehot, axis=0, keepdims=True)
    cnt_ref[...] = carry_scr[...]
    rt = jnp.zeros(work.shape, F32)
    for k in range(TOP_K):
        pk = jnp.sum(jnp.where(lane == eids[k], pos_full, 0.0), axis=1, keepdims=True)
        rt = jnp.where(lane == RT_GATE + k, ex[k] / den, rt)
        rt = jnp.where(lane == RT_EID + k, eids[k], rt)
        rt = jnp.where(lane == RT_POS + k, pk, rt)
    rt_ref[...] = rt


def _mix_out(x, oa, og, om, ln_g, ln_b, wo_bf16, g1, b1, wr_pad, br_pad, cnt0, h1_all, rt_all, row0, tm):
    n, d = x.shape
    off = row0 // tm
    assert row0 % tm == 0 and n % tm == 0
    tri = (jnp.arange(tm)[None, :] < jnp.arange(tm)[:, None]).astype(BF16)
    tok = lambda w: pl.BlockSpec((tm, w), lambda i: (i, 0))
    out = lambda w: pl.BlockSpec((tm, w), lambda i: (i + off, 0))
    full = lambda a: pl.BlockSpec(a.shape, lambda i: (0,) * a.ndim)
    any_spec = pl.BlockSpec(memory_space=pl.ANY)
    return pl.pallas_call(
        _mix_out_kernel,
        grid=(n // tm,),
        in_specs=[tok(d), tok(A_W), tok(G_V), tok(M_W), full(ln_g), full(ln_b), full(wo_bf16), full(g1), full(b1),
                  full(wr_pad), full(br_pad), full(tri), full(cnt0), any_spec, any_spec],
        out_specs=[out(d), out(LANES), pl.BlockSpec((1, LANES), lambda i: (0, 0))],
        out_shape=[jax.ShapeDtypeStruct(h1_all.shape, F32), jax.ShapeDtypeStruct(rt_all.shape, F32),
                   jax.ShapeDtypeStruct((1, LANES), F32)],
        input_output_aliases={13: 0, 14: 1},
        scratch_shapes=[pltpu.VMEM((1, LANES), F32)],
        compiler_params=_cparams(("arbitrary",)),
        name="mix_out",
    )(x, oa, og, om, ln_g, ln_b, wo_bf16, g1, b1, wr_pad, br_pad, tri, cnt0, h1_all, rt_all)


def _row_copy(src, s_row, dst, d_row, sem):
    return pltpu.make_async_copy(src.at[pl.ds(s_row, 1)], dst.at[pl.ds(d_row, 1)], sem)


def _dispatch_kernel(zs_ref, ze_ref, dest_ref, h1_ref, xs_hbm, zero_scr, sems):
    tm = h1_ref.shape[0]
    sem, zsem = sems.at[0], sems.at[1]

    def issue(r, c):
        for k in range(TOP_K):
            _row_copy(h1_ref, r, xs_hbm, dest_ref[r * TOP_K + k], sem).start()
        return c
    lax.fori_loop(0, tm, issue, 0, unroll=4)

    @pl.when(pl.program_id(0) == pl.num_programs(0) - 1)
    def _():
        zero_scr[...] = jnp.zeros(zero_scr.shape, F32)

        def zero_rows(wait):
            def per_row(r, c):
                cp = _row_copy(zero_scr, 0, xs_hbm, r, zsem)
                if wait:
                    cp.wait()
                else:
                    cp.start()
                return c

            def per_expert(e, c):
                return lax.fori_loop(zs_ref[e], ze_ref[e], per_row, c)
            lax.fori_loop(0, N_EXPERTS, per_expert, 0)
        zero_rows(False)
        zero_rows(True)

    def drain(r, c):
        for k in range(TOP_K):
            _row_copy(h1_ref, r, xs_hbm, 0, sem).wait()
        return c
    lax.fori_loop(0, tm, drain, 0, unroll=4)


def _dispatch(h1, dest_flat, z_start, z_end, n_rows, tm):
    n, d = h1.shape
    grid_spec = pltpu.PrefetchScalarGridSpec(
        num_scalar_prefetch=2,
        grid=(n // tm,),
        in_specs=[pl.BlockSpec((tm * TOP_K,), lambda i, zs, ze: (i,), memory_space=pltpu.SMEM),
                  pl.BlockSpec((tm, d), lambda i, zs, ze: (i, 0))],
        out_specs=pl.BlockSpec(memory_space=pl.ANY),
        scratch_shapes=[pltpu.VMEM((8, d), F32), pltpu.SemaphoreType.DMA((2,))],
    )
    return pl.pallas_call(
        _dispatch_kernel,
        grid_spec=grid_spec,
        out_shape=jax.ShapeDtypeStruct((n_rows, d), F32),
        compiler_params=_cparams(("arbitrary",)),
        name="moe_dispatch",
    )(z_start, z_end, dest_flat, h1)


_FF_CHUNK = 512


def _moe_kernel(be_ref, nu_ref, x_ref, wg_ref, bg_ref, wu_ref, bu_ref, wd_ref, bd_ref, y_ref,
                wgb_scr, wub_scr, wdb_scr):
    i = pl.program_id(0)
    prev = be_ref[jnp.maximum(i - 1, 0)]

    @pl.when((i == 0) | (be_ref[i] != prev))
    def _():
        wgb_scr[...] = wg_ref[0].astype(BF16)
        wub_scr[...] = wu_ref[0].astype(BF16)
        wdb_scr[...] = wd_ref[0].astype(BF16)

    @pl.when(i < nu_ref[0])
    def _():
        x = x_ref[...].astype(BF16)
        d_ff = wgb_scr.shape[1]
        y = jnp.zeros(y_ref.shape, F32) + bd_ref[0]
        for f0 in range(0, d_ff, _FF_CHUNK):
            f1 = f0 + _FF_CHUNK
            g = jnp.dot(x, wgb_scr[:, f0:f1], preferred_element_type=F32) + bg_ref[0, :, f0:f1]
            u = jnp.dot(x, wub_scr[:, f0:f1], preferred_element_type=F32) + bu_ref[0, :, f0:f1]
            g = jnp.minimum(g, SWIGLU_LIMIT)
            u = jnp.clip(u, -SWIGLU_LIMIT, SWIGLU_LIMIT)
            act = g * (1.0 / (1.0 + jnp.exp(-SWIGLU_ALPHA * g))) * (u + 1.0)
            y = y + jnp.dot(act.astype(BF16), wdb_scr[f0:f1, :], preferred_element_type=F32)
        y_ref[...] = y

    @pl.when(i >= nu_ref[0])
    def _():
        y_ref[...] = jnp.zeros(y_ref.shape, F32)


def _moe_experts(xs, block_e, n_used, wg, bg, wu, bu, wd, bd, tm):
    rows, d = xs.shape
    n_blocks = rows // tm
    d_ff = wg.shape[2]
    wspec = lambda a: pl.BlockSpec((1,) + a.shape[1:], lambda i, be, nu: (be[i], 0, 0))
    grid_spec = pltpu.PrefetchScalarGridSpec(
        num_scalar_prefetch=2,
        grid=(n_blocks,),
        in_specs=[pl.BlockSpec((tm, d), lambda i, be, nu: (jnp.minimum(i, nu[0] - 1), 0)),
                  wspec(wg), wspec(bg), wspec(wu), wspec(bu), wspec(wd), wspec(bd)],
        out_specs=pl.BlockSpec((tm, d), lambda i, be, nu: (i, 0)),
        scratch_shapes=[pltpu.VMEM((d, d_ff), BF16), pltpu.VMEM((d, d_ff), BF16), pltpu.VMEM((d_ff, d), BF16)],
    )
    return pl.pallas_call(
        _moe_kernel,
        grid_spec=grid_spec,
        out_shape=jax.ShapeDtypeStruct((rows, d), F32),
        compiler_params=_cparams(("arbitrary",)),
        name="moe_experts",
    )(block_e, n_used, xs, wg, bg, wu, bu, wd, bd)


def _combine_kernel(nt_a, dcur_ref, dnxt_ref, h1_ref, rt_ref, g_ref, b_ref, yb_hbm, oa_ref, ob_ref, gbuf, sem):
    i = pl.program_id(0)
    tm = h1_ref.shape[0]

    def gather(dref, slot, wait):
        def body(r, c):
            for k in range(TOP_K):
                cp = pltpu.make_async_copy(yb_hbm.at[pl.ds(dref[r * TOP_K + k], 1)],
                                           gbuf.at[slot, k, pl.ds(r, 1)], sem.at[slot])
                if wait:
                    cp.wait()
                else:
                    cp.start()
            return c
        lax.fori_loop(0, tm, body, 0, unroll=4)

    @pl.when(i == 0)
    def _():
        gather(dcur_ref, 0, False)

    @pl.when(i + 1 < pl.num_programs(0))
    def _():
        gather(dnxt_ref, (i + 1) % 2, False)

    slot = i % 2
    gather(dcur_ref, slot, True)
    rt = rt_ref[...]
    f = rt[:, RT_GATE:RT_GATE + 1] * gbuf[slot, 0]
    for k in range(1, TOP_K):
        f = f + rt[:, RT_GATE + k:RT_GATE + k + 1] * gbuf[slot, k]
    y = _layer_norm(DEEPNORM_ALPHA * h1_ref[...] + f, g_ref[...], b_ref[...])

    @pl.when(i < nt_a)
    def _():
        oa_ref[...] = y

    @pl.when(i >= nt_a)
    def _():
        ob_ref[...] = y


def _combine_ln2(h1, rt, dest_flat, yb, g, b, n_a, tm):
    n, d = h1.shape
    nt = n // tm
    nt_a = n_a // tm
    assert n_a % tm == 0 and 0 < nt_a < nt
    tok = lambda w: pl.BlockSpec((tm, w), lambda i: (i, 0))
    full = lambda a: pl.BlockSpec(a.shape, lambda i: (0,) * a.ndim)
    return pl.pallas_call(
        functools.partial(_combine_kernel, nt_a),
        grid=(nt,),
        in_specs=[pl.BlockSpec((tm * TOP_K,), lambda i: (i,), memory_space=pltpu.SMEM),
                  pl.BlockSpec((tm * TOP_K,), lambda i: (jnp.minimum(i + 1, nt - 1),), memory_space=pltpu.SMEM),
                  tok(d), tok(LANES), full(g), full(b), pl.BlockSpec(memory_space=pl.ANY)],
        out_specs=[pl.BlockSpec((tm, d), lambda i: (jnp.minimum(i, nt_a - 1), 0)),
                   pl.BlockSpec((tm, d), lambda i: (jnp.maximum(i - nt_a, 0), 0))],
        out_shape=[jax.ShapeDtypeStruct((n_a, d), F32), jax.ShapeDtypeStruct((n - n_a, d), F32)],
        scratch_shapes=[pltpu.VMEM((2, TOP_K, tm, d), F32), pltpu.SemaphoreType.DMA((2,))],
        compiler_params=_cparams(("arbitrary",)),
        name="moe_combine_ln2",
    )(dest_flat, dest_flat, h1, rt, g, b, yb)


def _layout(rt, counts, n, tm):
    counts = counts.astype(jnp.int32)
    padded = (counts + tm - 1) // tm * tm
    pad_end = jnp.cumsum(padded)
    pad_start = pad_end - padded
    eid = rt[:, RT_EID:RT_EID + TOP_K].astype(jnp.int32)
    pos = rt[:, RT_POS:RT_POS + TOP_K].astype(jnp.int32)
    dest = (pad_start[eid] + pos).reshape(-1)
    n_blocks = -(-(n * TOP_K) // tm) + N_EXPERTS
    first_row = jnp.arange(n_blocks, dtype=jnp.int32) * tm
    block_e = jnp.minimum(jnp.sum(pad_end[None, :] <= first_row[:, None], axis=1), N_EXPERTS - 1).astype(jnp.int32)
    n_used = (pad_end[-1] // tm).astype(jnp.int32).reshape(1)
    return dest, block_e, n_used, (pad_start + counts).astype(jnp.int32), pad_end.astype(jnp.int32), n_blocks * tm


def _hd_rows(a):
    nd = a.ndim
    t = jnp.transpose(a, tuple(range(nd - 3)) + (nd - 2, nd - 1, nd - 3))
    return t.reshape(t.shape[:-3] + (t.shape[-3] * t.shape[-2], t.shape[-1]))


def _rows_hd(at, heads):
    t = at.reshape(at.shape[:-2] + (heads, at.shape[-2] // heads, at.shape[-1]))
    nd = t.ndim
    return jnp.transpose(t, tuple(range(nd - 3)) + (nd - 1, nd - 3, nd - 2))


def kernel(x_prompt, x_sample, cache_k, cache_v, page_table, state_gla, cache_mem_k, cache_mem_v, mem_prompt,
           ln_in_g, ln_in_b, w_in, w_gla_gate, b_gla_gate, g_gla, w_mem_kv, w_out, ln1_g, ln1_b,
           w_router, b_router, w_gate, b_gate, w_up, b_up, w_down, b_down, ln2_g, ln2_b):
    assert w_in.shape[0] == DEPTH == 1
    bz, seq, d = x_prompt.shape
    db, t_new, _ = x_sample.shape
    page = cache_k.shape[2]
    past = page_table.shape[1] * page
    mem_len = mem_prompt.shape[1]
    n_p, n_s = bz * seq, db * t_new
    n = n_p + n_s
    tm = TOK_TM
    t_pad = 8
    row = lambda a: a.reshape(1, -1)

    wi = w_in[0]
    c_lg = 3 * A_W + 2 * G_K + G_V
    w_perm = jnp.concatenate([wi[:, :c_lg], wi[:, c_lg + G_LOWRANK:], wi[:, c_lg:c_lg + G_LOWRANK],
                              jnp.zeros((d, LANES - G_LOWRANK), F32)], axis=1).astype(BF16)
    wgg = jnp.stack(_hi_lo(jnp.zeros((LANES, G_K), F32).at[:G_LOWRANK].set(w_gla_gate[0])))
    bgg = row(b_gla_gate[0])
    gg = row(jnp.tile(g_gla[0], G_HEADS))
    wo = w_out[0].astype(BF16)
    wr32 = jnp.zeros((d, LANES), F32).at[:, :N_EXPERTS].set(w_router[0])
    wr_hi = wr32.astype(BF16)
    wr = jnp.stack([wr_hi, (wr32 - wr_hi.astype(F32)).astype(BF16)])
    br = jnp.full((1, LANES), -jnp.inf, F32).at[0, :N_EXPERTS].set(b_router[0])
    ln_g, ln_b = row(ln_in_g), row(ln_in_b)

    tabs_p = _rope_tables(np.arange(seq))
    tabs_s = _rope_tables(np.tile(past + np.arange(t_new), db))
    qp, ktp, vtp, gqp, gkp, gvp, lap, rgp, qmp = _project(x_prompt, tabs_p, ln_g, ln_b, w_perm, wgg, bgg, tm)
    xs3 = x_sample.reshape(1, n_s, d)
    qs, kts, vts, gqs, gks, gvs, las, rgs, qms = _project(xs3, tabs_s, ln_g, ln_b, w_perm, wgg, bgg, n_s)
    ks, vs = kts[0].T, vts[0].T
    mkt_p, mvt_p = _mem_kv(mem_prompt, w_mem_kv[0].astype(BF16))

    oa_p = _moba_prompt(qp, ktp, vtp)
    s0_p = jnp.zeros((bz, G_V, G_K), F32)
    og_p, st_p = _gla(gqp, gkp, lap, gvp, rgp, s0_p, gg, GLA_CHUNK, MOBA_BLOCK // GLA_CHUNK, True)
    om_p = _mem_attn(qmp, mkt_p, mvt_p, tm)

    pad_t = lambda a: jnp.pad(a.reshape(db, t_new, -1), ((0, 0), (0, t_pad - t_new), (0, 0)))
    oa_s = _moba_sample(pad_t(qs), pad_t(ks), pad_t(vs), _hd_rows(cache_k), _hd_rows(cache_v), page_table,
                        t_new)[:, :t_new]
    seq_per_step = 16
    grp = lambda a: pad_t(a).reshape(db // seq_per_step, seq_per_step * t_pad, -1)
    og_s, st_s = _gla(grp(gqs), grp(gks), grp(las), grp(gvs), grp(rgs), _state_to_t(state_gla[0]), gg,
                      t_pad, seq_per_step, False)
    og_s = og_s.reshape(db, t_pad, G_V)[:, :t_new]
    om_s = _mem_attn(pad_t(qms), _hd_rows(cache_mem_k[0]), _hd_rows(cache_mem_v[0]), t_pad)[:, :t_new]

    mix_w = (ln_g, ln_b, wo, row(ln1_g[0]), row(ln1_b[0]), wr, br)
    flat = lambda a, rows: a.reshape(rows, -1)
    h1, rt, cnt = _mix_out(flat(x_prompt, n_p), flat(oa_p, n_p), flat(og_p, n_p), flat(om_p, n_p), *mix_w,
                           jnp.zeros((1, LANES), F32), jnp.zeros((n, d), F32), jnp.zeros((n, LANES), F32), 0, tm)
    h1, rt, cnt = _mix_out(flat(x_sample, n_s), flat(oa_s, n_s), flat(og_s, n_s), flat(om_s, n_s), *mix_w,
                           cnt, h1, rt, n_p, tm)

    dest, block_e, n_used, z_start, z_end, n_rows = _layout(rt, cnt[0, :N_EXPERTS], n, tm)
    xs_sorted = _dispatch(h1, dest, z_start, z_end, n_rows, tm)
    b3 = lambda a: a[0].reshape(N_EXPERTS, 1, -1)
    yb = _moe_experts(xs_sorted, block_e, n_used, w_gate[0], b3(b_gate), w_up[0], b3(b_up), w_down[0], b3(b_down), tm)
    y_p, y_s = _combine_ln2(h1, rt, dest, yb, row(ln2_g[0]), row(ln2_b[0]), n_p, tm)

    return (y_p.reshape(bz, seq, d), y_s.reshape(db, t_new, d),
            _rows_hd(ktp, A_HEADS)[None], _rows_hd(vtp, A_HEADS)[None],
            _state_from_t(st_p)[None],
            _rows_hd(mkt_p, M_HEADS)[None], _rows_hd(mvt_p, M_HEADS)[None],
            ks.reshape(1, db, t_new, A_HEADS, A_DIM), vs.reshape(1, db, t_new, A_HEADS, A_DIM),
            _state_from_t(st_s)[None])
```

```python
import functools

import jax
import jax.numpy as jnp
import numpy as np
from jax import lax
from jax.experimental import pallas as pl
from jax.experimental.pallas import tpu as pltpu

F32 = jnp.float32
BF16 = jnp.bfloat16
HI = lax.Precision.HIGHEST
NT = (((1,), (1,)), ((), ()))

A_HEADS, A_DIM = 8, 64
ROT_DIM = A_DIM // 4
ROPE_THETA = 500000.0
MOBA_BLOCK, MOBA_TOPK = 256, 3
G_HEADS, G_DK, G_DV, G_LOWRANK, G_TAU = 4, 32, 64, 16, 16.0
M_HEADS, M_DIM = 4, 64
A_W = A_HEADS * A_DIM
G_K = G_HEADS * G_DK
G_V = G_HEADS * G_DV
M_W = M_HEADS * M_DIM
N_EXPERTS, TOP_K = 32, 4
SWIGLU_ALPHA, SWIGLU_LIMIT = 1.702, 7.0
LN_EPS, RMS_EPS = 1e-5, 1e-6
NEG = -1e30
LOG2E = 1.4426950408889634
DEPTH = 1
DEEPNORM_ALPHA = (2 * DEPTH) ** 0.25

LANES = 128
VMEM_LIMIT = 56 * 1024 * 1024

TOK_TM = 512
GLA_CHUNK = 16
RT_GATE, RT_EID, RT_POS = 0, TOP_K, 2 * TOP_K


def _cparams(sem):
    return pltpu.CompilerParams(dimension_semantics=sem, vmem_limit_bytes=VMEM_LIMIT)


def _hi_lo(x):
    hi = x.astype(BF16)
    return hi, (x - hi.astype(F32)).astype(BF16)


def _dot_exact_lhs(m, x):
    hi, lo = _hi_lo(x)
    return jnp.dot(m, hi, preferred_element_type=F32) + jnp.dot(m, lo, preferred_element_type=F32)


def _layer_norm(x, g, b):
    mu = jnp.mean(x, axis=-1, keepdims=True)
    xc = x - mu
    var = jnp.mean(xc * xc, axis=-1, keepdims=True)
    return xc * lax.rsqrt(var + LN_EPS) * g + b


_C_QA, _C_KA, _C_VA = 0, A_W, 2 * A_W
_C_QG = 3 * A_W
_C_KG = _C_QG + G_K
_C_VG = _C_KG + G_K
_C_RG = _C_VG + G_V
_C_QM = _C_RG + G_V
_C_LG = _C_QM + M_W
_C_END = _C_LG + LANES


def _proj_kernel(x_ref, g_ref, b_ref, w_ref, wgg_ref, bgg_ref, c_ref, s1_ref, s2_ref,
                 q_ref, kt_ref, vt_ref, gq_ref, gk_ref, gv_ref, la_ref, rg_ref, qm_ref):
    h = _layer_norm(x_ref[0], g_ref[...], b_ref[...]).astype(BF16)

    def mm(lo, hi):
        return jnp.dot(h, w_ref[:, lo:hi], preferred_element_type=F32)

    c, s1, s2 = (jnp.tile(t[...], (1, A_W // LANES)) for t in (c_ref, s1_ref, s2_ref))
    half = ROT_DIM // 2

    def rope(t):
        return t * c + pltpu.roll(t, A_W - half, 1) * s1 + pltpu.roll(t, half, 1) * s2

    q_ref[0] = rope(mm(_C_QA, _C_KA)) * (A_DIM ** -0.5 * LOG2E)
    kt_ref[0] = rope(mm(_C_KA, _C_VA)).T
    vt_ref[0] = mm(_C_VA, _C_QG).T
    gq_ref[0] = mm(_C_QG, _C_KG)
    gk_ref[0] = mm(_C_KG, _C_VG)
    gv_ref[0] = mm(_C_VG, _C_RG)
    rg_ref[0] = mm(_C_RG, _C_QM)
    qm_ref[0] = mm(_C_QM, _C_LG) * (M_DIM ** -0.5)
    lg_hi, lg_lo = _hi_lo(mm(_C_LG, _C_END))
    d3 = lambda a, w: jnp.dot(a, w, preferred_element_type=F32)
    z = d3(lg_hi, wgg_ref[0]) + (d3(lg_lo, wgg_ref[0]) + d3(lg_hi, wgg_ref[1])) + bgg_ref[...]
    la_ref[0] = (jnp.minimum(z, 0.0) - jnp.log(1.0 + jnp.exp(-jnp.abs(z)))) * (1.0 / G_TAU)


def _rope_tables(pos):
    half = ROT_DIM // 2
    f32 = np.float32
    inv = np.power(f32(ROPE_THETA), -np.arange(half, dtype=f32) / f32(half)).astype(f32)
    ang = pos.astype(f32)[:, None] * inv[None, :]
    cos, sin = np.cos(ang).astype(f32), np.sin(ang).astype(f32)
    n = pos.shape[0]
    one = np.ones((n, A_DIM - ROT_DIM), f32)
    zero8 = np.zeros((n, half), f32)
    zero = np.zeros((n, A_DIM - ROT_DIM), f32)
    c = np.concatenate([cos, cos, one], axis=1)
    s1 = np.concatenate([-sin, zero8, zero], axis=1)
    s2 = np.concatenate([zero8, sin, zero], axis=1)
    return tuple(jnp.asarray(np.tile(t, (1, LANES // A_DIM))) for t in (c, s1, s2))


def _project(x, pos_tables, ln_g, ln_b, w_perm, wgg, bgg, ts):
    bz, s, d = x.shape
    widths = (A_W, None, None, G_K, G_K, G_V, G_K, G_V, M_W)
    tok = lambda w: (pl.BlockSpec((1, ts, w), lambda j, b: (b, j, 0)) if w else
                     pl.BlockSpec((1, A_W, ts), lambda j, b: (b, 0, j)))
    full = lambda a: pl.BlockSpec(a.shape, lambda j, b: (0,) * a.ndim)
    tab = pl.BlockSpec((ts, LANES), lambda j, b: (j, 0))
    return pl.pallas_call(
        _proj_kernel,
        grid=(s // ts, bz),
        in_specs=[tok(d), full(ln_g), full(ln_b), full(w_perm), full(wgg), full(bgg), tab, tab, tab],
        out_specs=[tok(w) for w in widths],
        out_shape=[jax.ShapeDtypeStruct((bz, s, w) if w else (bz, A_W, s), F32) for w in widths],
        compiler_params=_cparams(("arbitrary", "arbitrary")),
        name="proj",
    )(x, ln_g, ln_b, w_perm, wgg, bgg, *pos_tables)


def _mem_kv_kernel(x_ref, w_ref, kt_ref, vt_ref):
    kv = jnp.dot(x_ref[0].astype(BF16), w_ref[...], preferred_element_type=F32)
    kt_ref[0] = kv[:, :M_W].T
    vt_ref[0] = kv[:, M_W:].T


def _mem_kv(mem, w_bf16):
    bz, m, d = mem.shape
    out = pl.BlockSpec((1, M_W, m), lambda b: (b, 0, 0))
    return pl.pallas_call(
        _mem_kv_kernel,
        grid=(bz,),
        in_specs=[pl.BlockSpec((1, m, d), lambda b: (b, 0, 0)), pl.BlockSpec(w_bf16.shape, lambda b: (0, 0))],
        out_specs=[out, out],
        out_shape=[jax.ShapeDtypeStruct((bz, M_W, m), F32)] * 2,
        compiler_params=_cparams(("arbitrary",)),
        name="mem_kv",
    )(mem, w_bf16)


_MOBA_HG = 4


_MOBA_VA = A_DIM + 16


def _moba_prompt_kernel(nblk, q_ref, kt_ref, vt_ref, o_ref, kb_scr, vt_scr, km_scr, selb_scr, acc_scr,
                        sa_scr, sb_scr, sd_scr):
    blk = MOBA_BLOCK
    hg = _MOBA_HG
    va = _MOBA_VA
    i = pl.program_id(2)

    @pl.when(i == 0)
    def _():
        lane = lax.broadcasted_iota(jnp.int32, (1, hg * A_DIM), 1)
        ones = jnp.ones((va - A_DIM, blk), BF16)
        for j in range(nblk):
            kj = kt_ref[0, :, j * blk:(j + 1) * blk].T
            kb_scr[j] = kj.astype(BF16)
            kmj = jnp.mean(kj, axis=0, keepdims=True)
            vj = vt_ref[0, :, j * blk:(j + 1) * blk].astype(BF16)
            for h in range(hg):
                km_scr[h * nblk + j:h * nblk + j + 1, :] = jnp.where(lane // A_DIM == h, kmj, 0.0)
                vt_scr[j, h * va:h * va + A_DIM, :] = vj[h * A_DIM:(h + 1) * A_DIM, :]
                vt_scr[j, h * va + A_DIM:(h + 1) * va, :] = ones

    q_t = q_ref[0].T
    row = lax.broadcasted_iota(jnp.int32, q_t.shape, 0)
    q_tb = [jnp.where((row >= A_DIM * h) & (row < A_DIM * (h + 1)), q_t, 0.0).astype(BF16) for h in range(hg)]
    km_hi, km_lo = _hi_lo(km_scr[...])
    q_hi, q_lo = _hi_lo(q_t)
    d3 = lambda a, w: jnp.dot(a, w, preferred_element_type=F32)
    g = (d3(km_hi, q_hi) + (d3(km_lo, q_hi) + d3(km_hi, q_lo))).reshape(hg, nblk, blk)
    blk_iota = lax.broadcasted_iota(jnp.int32, g.shape, 1)
    cnt = jnp.zeros(g.shape, jnp.int32)
    for jp in range(nblk):
        gj = g[:, jp:jp + 1, :]
        beats = (gj > g) | ((gj == g) & (jp < blk_iota))
        cnt = cnt + jnp.where(beats, 1, 0) * (jp < i).astype(jnp.int32)
    selb_scr[...] = jnp.where((blk_iota < i) & (cnt < MOBA_TOPK), 0.0, NEG)

    def scores(j, s_ref):
        kj = kb_scr[j]
        for h in range(hg):
            s_ref[h] = jnp.dot(kj, q_tb[h], preferred_element_type=F32)

    kpos = lax.broadcasted_iota(jnp.int32, (blk, blk), 0)
    qpos = lax.broadcasted_iota(jnp.int32, (blk, blk), 1)

    def absorb(j, s_ref, ms):
        vtj = vt_scr[j]
        new = []
        for h in range(hg):
            s = s_ref[h]
            if ms is None:
                s = jnp.where(kpos <= qpos, s, NEG)
                m_new = shift = jnp.max(s, axis=0, keepdims=True)
            else:
                bias = selb_scr[h, pl.ds(j, 1), :]
                m_new = jnp.maximum(ms[h], jnp.max(s, axis=0, keepdims=True) + bias)
                shift = m_new - bias
            pv = jnp.dot(vtj[h * va:(h + 1) * va, :], jnp.exp2(s - shift).astype(BF16), preferred_element_type=F32)
            acc_scr[h] = pv if ms is None else jnp.exp2(ms[h] - m_new) * acc_scr[h] + pv
            new.append(m_new)
        return tuple(new)

    last = nblk - 1
    scores(i, sd_scr)
    scores(0, sa_scr)
    ms = absorb(i, sd_scr, None)

    def past(jj, ms):
        ja, jb = 2 * jj, 2 * jj + 1
        scores(jb, sb_scr)
        ms = absorb(ja, sa_scr, ms)
        scores(jnp.minimum(ja + 2, last), sa_scr)
        return absorb(jb, sb_scr, ms)

    lax.fori_loop(0, (i + 1) // 2, past, ms)
    o_t = jnp.concatenate([acc_scr[h, :A_DIM, :] / acc_scr[h, A_DIM:A_DIM + 1, :] for h in range(hg)], axis=0)
    o_ref[0] = o_t.T


def _moba_prompt(q, kt, vt):
    bz, s, _ = q.shape
    blk = MOBA_BLOCK
    nblk = s // blk
    assert nblk % 2 == 0
    w = _MOBA_HG * A_DIM
    return pl.pallas_call(
        functools.partial(_moba_prompt_kernel, nblk),
        grid=(bz, A_W // w, nblk),
        in_specs=[pl.BlockSpec((1, blk, w), lambda b, hp, i: (b, i, hp)),
                  pl.BlockSpec((1, w, s), lambda b, hp, i: (b, hp, 0)),
                  pl.BlockSpec((1, w, s), lambda b, hp, i: (b, hp, 0))],
        out_specs=pl.BlockSpec((1, blk, w), lambda b, hp, i: (b, i, hp)),
        out_shape=jax.ShapeDtypeStruct((bz, s, A_W), F32),
        scratch_shapes=[pltpu.VMEM((nblk, blk, w), BF16), pltpu.VMEM((nblk, _MOBA_HG * _MOBA_VA, blk), BF16),
                        pltpu.VMEM((_MOBA_HG * nblk, w), F32), pltpu.VMEM((_MOBA_HG, nblk, blk), F32),
                        pltpu.VMEM((_MOBA_HG, _MOBA_VA, blk), F32)]
                       + [pltpu.VMEM((_MOBA_HG, blk, blk), F32)] * 3,
        compiler_params=_cparams(("arbitrary", "arbitrary", "arbitrary")),
        name="moba_prompt",
    )(q, kt, vt)


_PG_CHUNK = 8
_K_SLOTS = 8


def _moba_sample_kernel(t_new, n_pages, page, pt_ref, q_ref, kn_ref, vn_ref, ck_hbm, cv_hbm, o_ref,
                        kbuf, vbuf, ksem, vsem, fsem, s_scr, p_scr, bias_scr, flag_v, flag_s, unit_s):
    b = pl.program_id(0)
    nb = pl.num_programs(0)
    n_chunks = n_pages // _PG_CHUNK
    ppb = MOBA_BLOCK // page
    n_past = n_pages // ppb
    rows = t_new * A_HEADS

    def k_copy(bb, u, pg):
        slot = u % _K_SLOTS
        return pltpu.make_async_copy(ck_hbm.at[0, pt_ref[bb, u * _PG_CHUNK + pg]], kbuf.at[slot, pg], ksem.at[slot])

    def k_start(bb, u):
        for pg in range(_PG_CHUNK):
            k_copy(bb, u, pg).start()

    @pl.when(b == 0)
    def _():
        for u in range(_K_SLOTS):
            k_start(b, u)

        def zero(pg, c):
            vbuf[pg] = jnp.zeros(vbuf.shape[1:], F32)
            return c
        lax.fori_loop(0, n_pages, zero, 0)

    lane = lax.broadcasted_iota(jnp.int32, (A_HEADS, A_W), 1)
    sub = lax.broadcasted_iota(jnp.int32, (A_HEADS, A_W), 0)
    headmask = (lane // A_DIM == sub).astype(F32)
    q = q_ref[0]
    qbd = jnp.concatenate([jnp.broadcast_to(q[t:t + 1, :], (A_HEADS, A_W)) * headmask for t in range(t_new)],
                          axis=0)
    q_hi = qbd.astype(BF16)
    q_lo = (qbd - q_hi.astype(F32)).astype(BF16)
    q_hl = jnp.concatenate([q_hi, q_lo], axis=0)

    for u in range(n_chunks):
        for pg in range(_PG_CHUNK):
            k_copy(b, u, pg).wait()
        for pg in range(_PG_CHUNK):
            s2 = jnp.dot(q_hl, kbuf[u % _K_SLOTS, pg].astype(BF16), preferred_element_type=F32)
            s_scr[u * _PG_CHUNK + pg] = s2[:rows] + s2[rows:]
        ahead = u + _K_SLOTS
        if ahead < n_chunks:
            k_start(b, ahead)
        else:
            @pl.when(b + 1 < nb)
            def _():
                k_start(b + 1, ahead - n_chunks)

    col = lax.broadcasted_iota(jnp.int32, (rows, n_past), 1)
    g = jnp.zeros((rows, n_past), F32)
    for jb in range(n_past):
        blk_s = s_scr[jb * ppb]
        for e in range(1, ppb):
            blk_s = blk_s + s_scr[jb * ppb + e]
        g = jnp.where(col == jb, jnp.sum(blk_s, axis=1, keepdims=True), g)
    cnt = jnp.zeros((rows, n_past), jnp.int32)
    for jp in range(n_past):
        gj = g[:, jp:jp + 1]
        cnt = cnt + jnp.where((gj > g) | ((gj == g) & (jp < col)), 1, 0)
    sel = cnt < MOBA_TOPK

    picked = jnp.where(sel, 1, 0)
    head_any = picked[0:A_HEADS]
    for t in range(1, t_new):
        head_any = jnp.maximum(head_any, picked[t * A_HEADS:(t + 1) * A_HEADS])
    flag_v[...] = jnp.zeros(flag_v.shape, jnp.int32)
    flag_v[:, 0:n_past] = head_any
    flag_copy = pltpu.make_async_copy(flag_v, flag_s, fsem)
    flag_copy.start()
    flag_copy.wait()

    def collect(jb, n_units):
        for h in range(A_HEADS):
            unit_s[n_units] = jb * A_HEADS + h
            n_units = n_units + flag_s[h, jb]
        return n_units
    n_units = lax.fori_loop(0, n_past, collect, 0)

    def v_copies(idx):
        unit = unit_s[idx]
        jb = unit // A_HEADS
        r0 = pl.multiple_of((unit % A_HEADS) * A_DIM, A_DIM)
        return [pltpu.make_async_copy(cv_hbm.at[0, pt_ref[b, jb * ppb + e], pl.ds(r0, A_DIM), :],
                                      vbuf.at[jb * ppb + e, pl.ds(r0, A_DIM), :], vsem) for e in range(ppb)]

    def v_start(idx, c):
        for cp in v_copies(idx):
            cp.start(priority=1)
        return c
    lax.fori_loop(0, n_units, v_start, 0)


    selb = jnp.where(sel, 0.0, NEG)
    for jb in range(n_past):
        bias_scr[jb] = jnp.broadcast_to(selb[:, jb:jb + 1], (rows, page))
    kn, vn = kn_ref[0], vn_ref[0]
    r_t = lax.broadcasted_iota(jnp.int32, (rows, 1), 0) // A_HEADS
    s_own = [jnp.where(r_t >= t, jnp.sum(qbd * kn[t:t + 1, :], axis=1, keepdims=True), NEG) for t in range(t_new)]

    def smax(pg, mx):
        sb = s_scr[pg] + bias_scr[pg // ppb]
        s_scr[pg] = sb
        return jnp.maximum(mx, sb)
    mx = lax.fori_loop(0, n_pages, smax, jnp.full((rows, page), NEG, F32), unroll=4)
    m = jnp.max(mx, axis=1, keepdims=True)
    for t in range(t_new):
        m = jnp.maximum(m, s_own[t])

    def sexp(pg, ls):
        p = jnp.exp2(s_scr[pg] - m)
        p_scr[pg] = p.astype(BF16)
        return ls + p
    ls = lax.fori_loop(0, n_pages, sexp, jnp.zeros((rows, page), F32), unroll=4)
    l = jnp.sum(ls, axis=1, keepdims=True)
    acc = jnp.zeros((rows, A_W), F32)
    for t in range(t_new):
        p_t = jnp.exp2(s_own[t] - m)
        l = l + p_t
        acc = acc + p_t * vn[t:t + 1, :]

    def v_wait(idx, c):
        for cp in v_copies(idx):
            cp.wait()
        return c
    lax.fori_loop(0, n_units, v_wait, 0)

    def v_page(pg, a):
        return a + lax.dot_general(p_scr[pg], vbuf[pg].astype(BF16), NT, preferred_element_type=F32)
    acc = lax.fori_loop(0, n_pages, v_page, acc, unroll=8)

    o = acc / l
    out_rows = [jnp.sum(o[t * A_HEADS:(t + 1) * A_HEADS, :] * headmask, axis=0, keepdims=True)
                for t in range(t_new)]
    out_rows.append(jnp.zeros((o_ref.shape[1] - t_new, A_W), F32))
    o_ref[0] = jnp.concatenate(out_rows, axis=0)


def _moba_sample(q8, kn8, vn8, ck, cv, page_table, t_new):
    db, tp, _ = q8.shape
    n_pages = page_table.shape[1]
    page = ck.shape[3]
    assert (n_pages * page) % MOBA_BLOCK == 0, "past length must end on a MoBA block boundary"
    assert MOBA_BLOCK % page == 0 and n_pages % (_PG_CHUNK * _K_SLOTS) == 0
    n_past = n_pages * page // MOBA_BLOCK
    assert MOBA_TOPK <= n_past <= LANES
    rows = t_new * A_HEADS
    tok = pl.BlockSpec((1, tp, A_W), lambda b, pt: (b, 0, 0))
    any_spec = pl.BlockSpec(memory_space=pl.ANY)
    grid_spec = pltpu.PrefetchScalarGridSpec(
        num_scalar_prefetch=1,
        grid=(db,),
        in_specs=[tok, tok, tok, any_spec, any_spec],
        out_specs=tok,
        scratch_shapes=[pltpu.VMEM((_K_SLOTS, _PG_CHUNK, A_W, page), F32), pltpu.VMEM((n_pages, A_W, page), F32),
                        pltpu.SemaphoreType.DMA((_K_SLOTS,)), pltpu.SemaphoreType.DMA(()), pltpu.SemaphoreType.DMA(()),
                        pltpu.VMEM((n_pages, rows, page), F32), pltpu.VMEM((n_pages, rows, page), BF16),
                        pltpu.VMEM((n_past, rows, page), F32),
                        pltpu.VMEM((A_HEADS, LANES), jnp.int32), pltpu.SMEM((A_HEADS, LANES), jnp.int32),
                        pltpu.SMEM((n_past * A_HEADS,), jnp.int32)],
    )
    return pl.pallas_call(
        functools.partial(_moba_sample_kernel, t_new, n_pages, page),
        grid_spec=grid_spec,
        out_shape=jax.ShapeDtypeStruct((db, tp, A_W), F32),
        compiler_params=_cparams(("arbitrary",)),
        name="moba_sample",
    )(page_table, q8, kn8, vn8, ck, cv)


def _gla_kernel(c, nch, carry, q_ref, k_ref, la_ref, v_ref, rg_ref, s0_ref, gg_ref, l_ref, e_ref, hm_ref,
                bd_ref, hv_ref, o_ref, sf_ref, st_scr, b_scr, qs_scr, k_scr, qb_scr, dch_scr, oo_scr):
    j = pl.program_id(1)
    if carry:
        @pl.when(j == 0)
        def _():
            st_scr[...] = s0_ref[0]

    la = la_ref[0]
    b = _dot_exact_lhs(l_ref[...], la)
    tot = _dot_exact_lhs(e_ref[...], la)
    qs = q_ref[0] * (G_DK ** -0.5)
    k = k_ref[0]
    b_scr[...] = b
    qs_scr[...] = qs
    k_scr[...] = k
    qb_scr[...] = qs * jnp.exp(b)
    dch_scr[...] = jnp.exp(tot)
    kdec = (k * jnp.exp(tot - b)).astype(BF16)
    v_t = v_ref[0].T
    lane_chunk = lax.broadcasted_iota(jnp.int32, v_t.shape, 1) // c
    jrow = lax.broadcasted_iota(jnp.int32, (c, G_K), 0)

    def chunk(ci, st):
        r0 = pl.multiple_of(ci * c, c)
        if not carry:
            st = s0_ref[ci]
        o_inter = lax.dot_general(qb_scr[pl.ds(r0, c), :].astype(BF16), st.astype(BF16), NT,
                                  preferred_element_type=F32)
        b_c, k_c, qs_c = b_scr[pl.ds(r0, c), :], k_scr[pl.ds(r0, c), :], qs_scr[pl.ds(r0, c), :]
        terms = []
        for i in range(c):
            d = jnp.where(jrow <= i, b_c[i:i + 1, :] - b_c, NEG)
            terms.append(qs_c[i:i + 1, :] * k_c * jnp.exp(d))
        t_all = jnp.concatenate(terms, axis=0).astype(BF16)
        att = jnp.dot(t_all, hm_ref[...], preferred_element_type=F32)
        v_c = v_ref[0, pl.ds(r0, c), :]
        o_intra = jnp.sum(att.reshape(c, c, G_V) * v_c[None, :, :], axis=1)
        oo_scr[pl.ds(r0, c), :] = o_inter + o_intra
        v_m = jnp.where(lane_chunk == ci, v_t, 0.0).astype(BF16)
        upd = jnp.dot(v_m, kdec, preferred_element_type=F32) * bd_ref[...]
        st_new = st * dch_scr[pl.ds(r0, 1), :] + upd
        if not carry:
            sf_ref[ci] = st_new
        return st_new

    st = lax.fori_loop(0, nch, chunk, st_scr[...] if carry else jnp.zeros(st_scr.shape, F32))
    if carry:
        st_scr[...] = st

        @pl.when(j == pl.num_programs(1) - 1)
        def _():
            sf_ref[0] = st

    o = oo_scr[...]
    sq_hi, sq_lo = _hi_lo(o * o)
    hv = hv_ref[...]
    ms = jnp.dot(sq_hi, hv, preferred_element_type=F32) + jnp.dot(sq_lo, hv, preferred_element_type=F32)
    rg = rg_ref[0]
    o_ref[0] = o * lax.rsqrt(ms + RMS_EPS) * gg_ref[...] * (rg / (1.0 + jnp.exp(-rg)))


def _gla(gq, gk, la, gv, rg, s0_t, gg, c, nch, carry):
    bz, t, _ = gq.shape
    tt = c * nch
    idx = jnp.arange(tt)
    same = (idx[:, None] // c) == (idx[None, :] // c)
    l_mat = (same & (idx[None, :] <= idx[:, None])).astype(BF16)
    e_mat = same.astype(BF16)
    hm = ((jnp.arange(G_K)[:, None] // G_DK) == (jnp.arange(G_V)[None, :] // G_DV)).astype(BF16)
    bd = ((jnp.arange(G_V)[:, None] // G_DV) == (jnp.arange(G_K)[None, :] // G_DK)).astype(F32)
    hv = (((jnp.arange(G_V)[:, None] // G_DV) == (jnp.arange(G_V)[None, :] // G_DV)).astype(F32) / G_DV).astype(BF16)
    assert G_DV & (G_DV - 1) == 0, "1 / G_DV must be exact in bf16"
    tok = lambda w: pl.BlockSpec((1, tt, w), lambda b, j: (b, j, 0))
    full = lambda a: pl.BlockSpec(a.shape, lambda b, j: (0,) * a.ndim)
    ns = 1 if carry else nch
    st_spec = pl.BlockSpec((ns, G_V, G_K), lambda b, j: (b, 0, 0))
    return pl.pallas_call(
        functools.partial(_gla_kernel, c, nch, carry),
        grid=(bz, t // tt),
        in_specs=[tok(G_K), tok(G_K), tok(G_K), tok(G_V), tok(G_V), st_spec, full(gg), full(l_mat), full(e_mat),
                  full(hm), full(bd), full(hv)],
        out_specs=[tok(G_V), st_spec],
        out_shape=[jax.ShapeDtypeStruct((bz, t, G_V), F32), jax.ShapeDtypeStruct(s0_t.shape, F32)],
        scratch_shapes=[pltpu.VMEM((G_V, G_K), F32)] + [pltpu.VMEM((tt, G_K), F32)] * 5
                       + [pltpu.VMEM((tt, G_V), F32)],
        compiler_params=_cparams(("arbitrary", "arbitrary")),
        name="gla",
    )(gq, gk, la, gv, rg, s0_t, gg, l_mat, e_mat, hm, bd, hv)


def _state_to_t(s):
    n = s.shape[0]
    eye = jnp.eye(G_HEADS, dtype=s.dtype)
    return jnp.einsum("nhdv,hg->nhvgd", s, eye).reshape(n, G_V, G_K)


def _state_from_t(st):
    n = st.shape[0]
    s5 = st.reshape(n, G_HEADS, G_DV, G_HEADS, G_DK)
    return jnp.stack([s5[:, h, :, h, :] for h in range(G_HEADS)], axis=1).transpose(0, 1, 3, 2)


def _mem_attn_kernel(q_ref, mkt_ref, mvt_ref, o_ref):
    q = q_ref[0]
    mkt = mkt_ref[0].astype(BF16)
    mvt = mvt_ref[0].astype(BF16)
    lane = lax.broadcasted_iota(jnp.int32, q.shape, 1)
    out = jnp.zeros(q.shape, F32)
    for h in range(M_HEADS):
        hsel = (lane // M_DIM) == h
        s = jnp.dot(jnp.where(hsel, q, 0.0).astype(BF16), mkt, preferred_element_type=F32)
        m = jnp.max(s, axis=1, keepdims=True)
        p = jnp.exp(s - m)
        l = jnp.sum(p, axis=1, keepdims=True)
        o_h = lax.dot_general(p.astype(BF16), mvt, NT, preferred_element_type=F32)
        out = out + jnp.where(hsel, o_h / l, 0.0)
    o_ref[0] = out


def _mem_attn(qm, mkt, mvt, tm):
    bz, t, _ = qm.shape
    mem = mkt.shape[2]
    return pl.pallas_call(
        _mem_attn_kernel,
        grid=(bz, t // tm),
        in_specs=[pl.BlockSpec((1, tm, M_W), lambda b, j: (b, j, 0)),
                  pl.BlockSpec((1, M_W, mem), lambda b, j: (b, 0, 0)),
                  pl.BlockSpec((1, M_W, mem), lambda b, j: (b, 0, 0))],
        out_specs=pl.BlockSpec((1, tm, M_W), lambda b, j: (b, j, 0)),
        out_shape=jax.ShapeDtypeStruct((bz, t, M_W), F32),
        compiler_params=_cparams(("arbitrary", "arbitrary")),
        name="mem_attn",
    )(qm, mkt, mvt)


def _mix_out_kernel(x_ref, oa_ref, og_ref, om_ref, lng_ref, lnb_ref, wo_ref, g1_ref, b1_ref, wr_ref, br_ref,
                    tri_ref, cnt0_ref, h1_all, rt_all, h1_ref, rt_ref, cnt_ref, carry_scr):
    del h1_all, rt_all

    @pl.when(pl.program_id(0) == 0)
    def _():
        carry_scr[...] = cnt0_ref[...]

    h = _layer_norm(x_ref[...], lng_ref[...], lnb_ref[...])
    o = jnp.dot(oa_ref[...].astype(BF16), wo_ref[0:A_W, :], preferred_element_type=F32)
    o = o + jnp.dot(og_ref[...].astype(BF16), wo_ref[A_W:A_W + G_V, :], preferred_element_type=F32)
    o = o + jnp.dot(om_ref[...].astype(BF16), wo_ref[A_W + G_V:, :], preferred_element_type=F32)
    h1 = _layer_norm(DEEPNORM_ALPHA * h + o, g1_ref[...], b1_ref[...])
    h1_ref[...] = h1
    h1_hi = h1.astype(BF16)
    h1_lo = (h1 - h1_hi.astype(F32)).astype(BF16)
    mm = lambda a, w: jnp.dot(a, w, preferred_element_type=F32)
    work = mm(h1_hi, wr_ref[0]) + (mm(h1_lo, wr_ref[0]) + mm(h1_hi, wr_ref[1])) + br_ref[...]
    lane = lax.broadcasted_iota(jnp.int32, work.shape, 1).astype(F32)
    onehot = jnp.zeros(work.shape, F32)
    vals, eids = [], []
    for _ in range(TOP_K):
        mk = jnp.max(work, axis=1, keepdims=True)
        ek = jnp.min(jnp.where(work == mk, lane, float(LANES)), axis=1, keepdims=True)
        hit = lane == ek
        onehot = jnp.where(hit, 1.0, onehot)
        work = jnp.where(hit, -jnp.inf, work)
        vals.append(mk)
        eids.append(ek)
    ex = [jnp.exp(v - vals[0]) for v in vals]
    den = ex[0] + ex[1] + ex[2] + ex[3]
    pos_full = jnp.dot(tri_ref[...], onehot.astype(BF16), preferred_element_type=F32) + carry_scr[...]
    carry_scr[...] = carry_scr[...] + jnp.sum(onehot, axis=0, keepdims=True)
    cnt_ref[...] = carry_scr[...]
    rt = jnp.zeros(work.shape, F32)
    for k in range(TOP_K):
        pk = jnp.sum(jnp.where(lane == eids[k], pos_full, 0.0), axis=1, keepdims=True)
        rt = jnp.where(lane == RT_GATE + k, ex[k] / den, rt)
        rt = jnp.where(lane == RT_EID + k, eids[k], rt)
        rt = jnp.where(lane == RT_POS + k, pk, rt)
    rt_ref[...] = rt


def _mix_out(x, oa, og, om, ln_g, ln_b, wo_bf16, g1, b1, wr_pad, br_pad, cnt0, h1_all, rt_all, row0, tm):
    n, d = x.shape
    off = row0 // tm
    assert row0 % tm == 0 and n % tm == 0
    tri = (jnp.arange(tm)[None, :] < jnp.arange(tm)[:, None]).astype(BF16)
    tok = lambda w: pl.BlockSpec((tm, w), lambda i: (i, 0))
    out = lambda w: pl.BlockSpec((tm, w), lambda i: (i + off, 0))
    full = lambda a: pl.BlockSpec(a.shape, lambda i: (0,) * a.ndim)
    any_spec = pl.BlockSpec(memory_space=pl.ANY)
    return pl.pallas_call(
        _mix_out_kernel,
        grid=(n // tm,),
        in_specs=[tok(d), tok(A_W), tok(G_V), tok(M_W), full(ln_g), full(ln_b), full(wo_bf16), full(g1), full(b1),
                  full(wr_pad), full(br_pad), full(tri), full(cnt0), any_spec, any_spec],
        out_specs=[out(d), out(LANES), pl.BlockSpec((1, LANES), lambda i: (0, 0))],
        out_shape=[jax.ShapeDtypeStruct(h1_all.shape, F32), jax.ShapeDtypeStruct(rt_all.shape, F32),
                   jax.ShapeDtypeStruct((1, LANES), F32)],
        input_output_aliases={13: 0, 14: 1},
        scratch_shapes=[pltpu.VMEM((1, LANES), F32)],
        compiler_params=_cparams(("arbitrary",)),
        name="mix_out",
    )(x, oa, og, om, ln_g, ln_b, wo_bf16, g1, b1, wr_pad, br_pad, tri, cnt0, h1_all, rt_all)


def _row_copy(src, s_row, dst, d_row, sem):
    return pltpu.make_async_copy(src.at[pl.ds(s_row, 1)], dst.at[pl.ds(d_row, 1)], sem)


def _dispatch_kernel(zs_ref, ze_ref, dest_ref, h1_ref, xs_hbm, zero_scr, sems):
    tm = h1_ref.shape[0]
    sem, zsem = sems.at[0], sems.at[1]

    def issue(r, c):
        for k in range(TOP_K):
            _row_copy(h1_ref, r, xs_hbm, dest_ref[r * TOP_K + k], sem).start()
        return c
    lax.fori_loop(0, tm, issue, 0, unroll=4)

    @pl.when(pl.program_id(0) == pl.num_programs(0) - 1)
    def _():
        zero_scr[...] = jnp.zeros(zero_scr.shape, F32)

        def zero_rows(wait):
            def per_row(r, c):
                cp = _row_copy(zero_scr, 0, xs_hbm, r, zsem)
                if wait:
                    cp.wait()
                else:
                    cp.start()
                return c

            def per_expert(e, c):
                return lax.fori_loop(zs_ref[e], ze_ref[e], per_row, c)
            lax.fori_loop(0, N_EXPERTS, per_expert, 0)
        zero_rows(False)
        zero_rows(True)

    def drain(r, c):
        for k in range(TOP_K):
            _row_copy(h1_ref, r, xs_hbm, 0, sem).wait()
        return c
    lax.fori_loop(0, tm, drain, 0, unroll=4)


def _dispatch(h1, dest_flat, z_start, z_end, n_rows, tm):
    n, d = h1.shape
    grid_spec = pltpu.PrefetchScalarGridSpec(
        num_scalar_prefetch=2,
        grid=(n // tm,),
        in_specs=[pl.BlockSpec((tm * TOP_K,), lambda i, zs, ze: (i,), memory_space=pltpu.SMEM),
                  pl.BlockSpec((tm, d), lambda i, zs, ze: (i, 0))],
        out_specs=pl.BlockSpec(memory_space=pl.ANY),
        scratch_shapes=[pltpu.VMEM((8, d), F32), pltpu.SemaphoreType.DMA((2,))],
    )
    return pl.pallas_call(
        _dispatch_kernel,
        grid_spec=grid_spec,
        out_shape=jax.ShapeDtypeStruct((n_rows, d), F32),
        compiler_params=_cparams(("arbitrary",)),
        name="moe_dispatch",
    )(z_start, z_end, dest_flat, h1)


_FF_CHUNK = 512


def _moe_kernel(be_ref, nu_ref, x_ref, wg_ref, bg_ref, wu_ref, bu_ref, wd_ref, bd_ref, y_ref,
                wgb_scr, wub_scr, wdb_scr):
    i = pl.program_id(0)
    prev = be_ref[jnp.maximum(i - 1, 0)]

    @pl.when((i == 0) | (be_ref[i] != prev))
    def _():
        wgb_scr[...] = wg_ref[0].astype(BF16)
        wub_scr[...] = wu_ref[0].astype(BF16)
        wdb_scr[...] = wd_ref[0].astype(BF16)

    @pl.when(i < nu_ref[0])
    def _():
        x = x_ref[...].astype(BF16)
        d_ff = wgb_scr.shape[1]
        y = jnp.zeros(y_ref.shape, F32) + bd_ref[0]
        for f0 in range(0, d_ff, _FF_CHUNK):
            f1 = f0 + _FF_CHUNK
            g = jnp.dot(x, wgb_scr[:, f0:f1], preferred_element_type=F32) + bg_ref[0, :, f0:f1]
            u = jnp.dot(x, wub_scr[:, f0:f1], preferred_element_type=F32) + bu_ref[0, :, f0:f1]
            g = jnp.minimum(g, SWIGLU_LIMIT)
            u = jnp.clip(u, -SWIGLU_LIMIT, SWIGLU_LIMIT)
            act = g * (1.0 / (1.0 + jnp.exp(-SWIGLU_ALPHA * g))) * (u + 1.0)
            y = y + jnp.dot(act.astype(BF16), wdb_scr[f0:f1, :], preferred_element_type=F32)
        y_ref[...] = y

    @pl.when(i >= nu_ref[0])
    def _():
        y_ref[...] = jnp.zeros(y_ref.shape, F32)


def _moe_experts(xs, block_e, n_used, wg, bg, wu, bu, wd, bd, tm):
    rows, d = xs.shape
    n_blocks = rows // tm
    d_ff = wg.shape[2]
    wspec = lambda a: pl.BlockSpec((1,) + a.shape[1:], lambda i, be, nu: (be[i], 0, 0))
    grid_spec = pltpu.PrefetchScalarGridSpec(
        num_scalar_prefetch=2,
        grid=(n_blocks,),
        in_specs=[pl.BlockSpec((tm, d), lambda i, be, nu: (jnp.minimum(i, nu[0] - 1), 0)),
                  wspec(wg), wspec(bg), wspec(wu), wspec(bu), wspec(wd), wspec(bd)],
        out_specs=pl.BlockSpec((tm, d), lambda i, be, nu: (i, 0)),
        scratch_shapes=[pltpu.VMEM((d, d_ff), BF16), pltpu.VMEM((d, d_ff), BF16), pltpu.VMEM((d_ff, d), BF16)],
    )
    return pl.pallas_call(
        _moe_kernel,
        grid_spec=grid_spec,
        out_shape=jax.ShapeDtypeStruct((rows, d), F32),
        compiler_params=_cparams(("arbitrary",)),
        name="moe_experts",
    )(block_e, n_used, xs, wg, bg, wu, bu, wd, bd)


def _combine_kernel(nt_a, dcur_ref, dnxt_ref, h1_ref, rt_ref, g_ref, b_ref, yb_hbm, oa_ref, ob_ref, gbuf, sem):
    i = pl.program_id(0)
    tm = h1_ref.shape[0]

    def gather(dref, slot, wait):
        def body(r, c):
            for k in range(TOP_K):
                cp = pltpu.make_async_copy(yb_hbm.at[pl.ds(dref[r * TOP_K + k], 1)],
                                           gbuf.at[slot, k, pl.ds(r, 1)], sem.at[slot])
                if wait:
                    cp.wait()
                else:
                    cp.start()
            return c
        lax.fori_loop(0, tm, body, 0, unroll=4)

    @pl.when(i == 0)
    def _():
        gather(dcur_ref, 0, False)

    @pl.when(i + 1 < pl.num_programs(0))
    def _():
        gather(dnxt_ref, (i + 1) % 2, False)

    slot = i % 2
    gather(dcur_ref, slot, True)
    rt = rt_ref[...]
    f = rt[:, RT_GATE:RT_GATE + 1] * gbuf[slot, 0]
    for k in range(1, TOP_K):
        f = f + rt[:, RT_GATE + k:RT_GATE + k + 1] * gbuf[slot, k]
    y = _layer_norm(DEEPNORM_ALPHA * h1_ref[...] + f, g_ref[...], b_ref[...])

    @pl.when(i < nt_a)
    def _():
        oa_ref[...] = y

    @pl.when(i >= nt_a)
    def _():
        ob_ref[...] = y


def _combine_ln2(h1, rt, dest_flat, yb, g, b, n_a, tm):
    n, d = h1.shape
    nt = n // tm
    nt_a = n_a // tm
    assert n_a % tm == 0 and 0 < nt_a < nt
    tok = lambda w: pl.BlockSpec((tm, w), lambda i: (i, 0))
    full = lambda a: pl.BlockSpec(a.shape, lambda i: (0,) * a.ndim)
    return pl.pallas_call(
        functools.partial(_combine_kernel, nt_a),
        grid=(nt,),
        in_specs=[pl.BlockSpec((tm * TOP_K,), lambda i: (i,), memory_space=pltpu.SMEM),
                  pl.BlockSpec((tm * TOP_K,), lambda i: (jnp.minimum(i + 1, nt - 1),), memory_space=pltpu.SMEM),
                  tok(d), tok(LANES), full(g), full(b), pl.BlockSpec(memory_space=pl.ANY)],
        out_specs=[pl.BlockSpec((tm, d), lambda i: (jnp.minimum(i, nt_a - 1), 0)),
                   pl.BlockSpec((tm, d), lambda i: (jnp.maximum(i - nt_a, 0), 0))],
        out_shape=[jax.ShapeDtypeStruct((n_a, d), F32), jax.ShapeDtypeStruct((n - n_a, d), F32)],
        scratch_shapes=[pltpu.VMEM((2, TOP_K, tm, d), F32), pltpu.SemaphoreType.DMA((2,))],
        compiler_params=_cparams(("arbitrary",)),
        name="moe_combine_ln2",
    )(dest_flat, dest_flat, h1, rt, g, b, yb)


def _layout(rt, counts, n, tm):
    counts = counts.astype(jnp.int32)
    padded = (counts + tm - 1) // tm * tm
    pad_end = jnp.cumsum(padded)
    pad_start = pad_end - padded
    eid = rt[:, RT_EID:RT_EID + TOP_K].astype(jnp.int32)
    pos = rt[:, RT_POS:RT_POS + TOP_K].astype(jnp.int32)
    first = jnp.sum(jnp.where(eid[..., None] == jnp.arange(N_EXPERTS), pad_start, 0), axis=-1)
    dest = (first + pos).reshape(-1)
    n_blocks = -(-(n * TOP_K) // tm) + N_EXPERTS
    first_row = jnp.arange(n_blocks, dtype=jnp.int32) * tm
    block_e = jnp.minimum(jnp.sum(pad_end[None, :] <= first_row[:, None], axis=1), N_EXPERTS - 1).astype(jnp.int32)
    n_used = (pad_end[-1] // tm).astype(jnp.int32).reshape(1)
    return dest, block_e, n_used, (pad_start + counts).astype(jnp.int32), pad_end.astype(jnp.int32), n_blocks * tm


def _hd_rows(a):
    nd = a.ndim
    t = jnp.transpose(a, tuple(range(nd - 3)) + (nd - 2, nd - 1, nd - 3))
    return t.reshape(t.shape[:-3] + (t.shape[-3] * t.shape[-2], t.shape[-1]))


def _rows_hd(at, heads):
    t = at.reshape(at.shape[:-2] + (heads, at.shape[-2] // heads, at.shape[-1]))
    nd = t.ndim
    return jnp.transpose(t, tuple(range(nd - 3)) + (nd - 1, nd - 3, nd - 2))


def kernel(x_prompt, x_sample, cache_k, cache_v, page_table, state_gla, cache_mem_k, cache_mem_v, mem_prompt,
           ln_in_g, ln_in_b, w_in, w_gla_gate, b_gla_gate, g_gla, w_mem_kv, w_out, ln1_g, ln1_b,
           w_router, b_router, w_gate, b_gate, w_up, b_up, w_down, b_down, ln2_g, ln2_b):
    assert w_in.shape[0] == DEPTH == 1
    bz, seq, d = x_prompt.shape
    db, t_new, _ = x_sample.shape
    page = cache_k.shape[2]
    past = page_table.shape[1] * page
    mem_len = mem_prompt.shape[1]
    n_p, n_s = bz * seq, db * t_new
    n = n_p + n_s
    tm = TOK_TM
    t_pad = 8
    row = lambda a: a.reshape(1, -1)

    wi = w_in[0]
    c_lg = 3 * A_W + 2 * G_K + G_V
    w_perm = jnp.concatenate([wi[:, :c_lg], wi[:, c_lg + G_LOWRANK:], wi[:, c_lg:c_lg + G_LOWRANK],
                              jnp.zeros((d, LANES - G_LOWRANK), F32)], axis=1).astype(BF16)
    wgg = jnp.stack(_hi_lo(jnp.zeros((LANES, G_K), F32).at[:G_LOWRANK].set(w_gla_gate[0])))
    bgg = row(b_gla_gate[0])
    gg = row(jnp.tile(g_gla[0], G_HEADS))
    wo = w_out[0].astype(BF16)
    wr32 = jnp.zeros((d, LANES), F32).at[:, :N_EXPERTS].set(w_router[0])
    wr_hi = wr32.astype(BF16)
    wr = jnp.stack([wr_hi, (wr32 - wr_hi.astype(F32)).astype(BF16)])
    br = jnp.full((1, LANES), -jnp.inf, F32).at[0, :N_EXPERTS].set(b_router[0])
    ln_g, ln_b = row(ln_in_g), row(ln_in_b)

    tabs_p = _rope_tables(np.arange(seq))
    tabs_s = _rope_tables(np.tile(past + np.arange(t_new), db))
    qp, ktp, vtp, gqp, gkp, gvp, lap, rgp, qmp = _project(x_prompt, tabs_p, ln_g, ln_b, w_perm, wgg, bgg, tm)
    xs3 = x_sample.reshape(1, n_s, d)
    qs, kts, vts, gqs, gks, gvs, las, rgs, qms = _project(xs3, tabs_s, ln_g, ln_b, w_perm, wgg, bgg, n_s)
    ks, vs = kts[0].T, vts[0].T
    mkt_p, mvt_p = _mem_kv(mem_prompt, w_mem_kv[0].astype(BF16))

    oa_p = _moba_prompt(qp, ktp, vtp)
    s0_p = jnp.zeros((bz, G_V, G_K), F32)
    og_p, st_p = _gla(gqp, gkp, lap, gvp, rgp, s0_p, gg, GLA_CHUNK, MOBA_BLOCK // GLA_CHUNK, True)
    om_p = _mem_attn(qmp, mkt_p, mvt_p, tm)

    pad_t = lambda a: jnp.pad(a.reshape(db, t_new, -1), ((0, 0), (0, t_pad - t_new), (0, 0)))
    oa_s = _moba_sample(pad_t(qs), pad_t(ks), pad_t(vs), _hd_rows(cache_k), _hd_rows(cache_v), page_table,
                        t_new)[:, :t_new]
    seq_per_step = 16
    grp = lambda a: pad_t(a).reshape(db // seq_per_step, seq_per_step * t_pad, -1)
    og_s, st_s = _gla(grp(gqs), grp(gks), grp(las), grp(gvs), grp(rgs), _state_to_t(state_gla[0]), gg,
                      t_pad, seq_per_step, False)
    og_s = og_s.reshape(db, t_pad, G_V)[:, :t_new]
    om_s = _mem_attn(pad_t(qms), _hd_rows(cache_mem_k[0]), _hd_rows(cache_mem_v[0]), t_pad)[:, :t_new]

    mix_w = (ln_g, ln_b, wo, row(ln1_g[0]), row(ln1_b[0]), wr, br)
    flat = lambda a, rows: a.reshape(rows, -1)
    h1, rt, cnt = _mix_out(flat(x_prompt, n_p), flat(oa_p, n_p), flat(og_p, n_p), flat(om_p, n_p), *mix_w,
                           jnp.zeros((1, LANES), F32), jnp.zeros((n, d), F32), jnp.zeros((n, LANES), F32), 0, tm)
    h1, rt, cnt = _mix_out(flat(x_sample, n_s), flat(oa_s, n_s), flat(og_s, n_s), flat(om_s, n_s), *mix_w,
                           cnt, h1, rt, n_p, tm)

    dest, block_e, n_used, z_start, z_end, n_rows = _layout(rt, cnt[0, :N_EXPERTS], n, tm)
    xs_sorted = _dispatch(h1, dest, z_start, z_end, n_rows, tm)
    b3 = lambda a: a[0].reshape(N_EXPERTS, 1, -1)
    yb = _moe_experts(xs_sorted, block_e, n_used, w_gate[0], b3(b_gate), w_up[0], b3(b_up), w_down[0], b3(b_down), tm)
    y_p, y_s = _combine_ln2(h1, rt, dest, yb, row(ln2_g[0]), row(ln2_b[0]), n_p, tm)

    return (y_p.reshape(bz, seq, d), y_s.reshape(db, t_new, d),
            _rows_hd(ktp, A_HEADS)[None], _rows_hd(vtp, A_HEADS)[None],
            _state_from_t(st_p)[None],
            _rows_hd(mkt_p, M_HEADS)[None], _rows_hd(mvt_p, M_HEADS)[None],
            ks.reshape(1, db, t_new, A_HEADS, A_DIM), vs.reshape(1, db, t_new, A_HEADS, A_DIM),
            _state_from_t(st_s)[None])
```

```python
import functools

import jax
import jax.numpy as jnp
import numpy as np
from jax import lax
from jax.experimental import pallas as pl
from jax.experimental.pallas import tpu as pltpu

F32 = jnp.float32
BF16 = jnp.bfloat16
HI = lax.Precision.HIGHEST
NT = (((1,), (1,)), ((), ()))

A_HEADS, A_DIM = 8, 64
ROT_DIM = A_DIM // 4
ROPE_THETA = 500000.0
MOBA_BLOCK, MOBA_TOPK = 256, 3
G_HEADS, G_DK, G_DV, G_LOWRANK, G_TAU = 4, 32, 64, 16, 16.0
M_HEADS, M_DIM = 4, 64
A_W = A_HEADS * A_DIM
G_K = G_HEADS * G_DK
G_V = G_HEADS * G_DV
M_W = M_HEADS * M_DIM
N_EXPERTS, TOP_K = 32, 4
SWIGLU_ALPHA, SWIGLU_LIMIT = 1.702, 7.0
LN_EPS, RMS_EPS = 1e-5, 1e-6
NEG = -1e30
LOG2E = 1.4426950408889634
DEPTH = 1
DEEPNORM_ALPHA = (2 * DEPTH) ** 0.25

LANES = 128
VMEM_LIMIT = 56 * 1024 * 1024

TOK_TM = 512
GLA_CHUNK = 16
RT_GATE, RT_EID, RT_POS = 0, TOP_K, 2 * TOP_K


def _cparams(sem):
    return pltpu.CompilerParams(dimension_semantics=sem, vmem_limit_bytes=VMEM_LIMIT)


def _hi_lo(x):
    hi = x.astype(BF16)
    return hi, (x - hi.astype(F32)).astype(BF16)


def _dot_exact_lhs(m, x):
    hi, lo = _hi_lo(x)
    return jnp.dot(m, hi, preferred_element_type=F32) + jnp.dot(m, lo, preferred_element_type=F32)


def _layer_norm(x, g, b):
    mu = jnp.mean(x, axis=-1, keepdims=True)
    xc = x - mu
    var = jnp.mean(xc * xc, axis=-1, keepdims=True)
    return xc * lax.rsqrt(var + LN_EPS) * g + b


_C_QA, _C_KA, _C_VA = 0, A_W, 2 * A_W
_C_QG = 3 * A_W
_C_KG = _C_QG + G_K
_C_VG = _C_KG + G_K
_C_RG = _C_VG + G_V
_C_QM = _C_RG + G_V
_C_LG = _C_QM + M_W
_C_END = _C_LG + LANES


def _proj_kernel(x_ref, g_ref, b_ref, w_ref, wgg_ref, bgg_ref, c_ref, s1_ref, s2_ref,
                 q_ref, kt_ref, vt_ref, gq_ref, gk_ref, gv_ref, la_ref, rg_ref, qm_ref):
    h = _layer_norm(x_ref[0], g_ref[...], b_ref[...]).astype(BF16)

    def mm(lo, hi):
        return jnp.dot(h, w_ref[:, lo:hi], preferred_element_type=F32)

    c, s1, s2 = (jnp.tile(t[...], (1, A_W // LANES)) for t in (c_ref, s1_ref, s2_ref))
    half = ROT_DIM // 2

    def rope(t):
        return t * c + pltpu.roll(t, A_W - half, 1) * s1 + pltpu.roll(t, half, 1) * s2

    q_ref[0] = rope(mm(_C_QA, _C_KA)) * (A_DIM ** -0.5 * LOG2E)
    kt_ref[0] = rope(mm(_C_KA, _C_VA)).T
    vt_ref[0] = mm(_C_VA, _C_QG).T
    gq_ref[0] = mm(_C_QG, _C_KG)
    gk_ref[0] = mm(_C_KG, _C_VG)
    gv_ref[0] = mm(_C_VG, _C_RG)
    rg_ref[0] = mm(_C_RG, _C_QM)
    qm_ref[0] = mm(_C_QM, _C_LG) * (M_DIM ** -0.5)
    lg_hi, lg_lo = _hi_lo(mm(_C_LG, _C_END))
    d3 = lambda a, w: jnp.dot(a, w, preferred_element_type=F32)
    z = d3(lg_hi, wgg_ref[0]) + (d3(lg_lo, wgg_ref[0]) + d3(lg_hi, wgg_ref[1])) + bgg_ref[...]
    la_ref[0] = (jnp.minimum(z, 0.0) - jnp.log(1.0 + jnp.exp(-jnp.abs(z)))) * (1.0 / G_TAU)


def _rope_tables(pos):
    half = ROT_DIM // 2
    f32 = np.float32
    inv = np.power(f32(ROPE_THETA), -np.arange(half, dtype=f32) / f32(half)).astype(f32)
    ang = pos.astype(f32)[:, None] * inv[None, :]
    cos, sin = np.cos(ang).astype(f32), np.sin(ang).astype(f32)
    n = pos.shape[0]
    one = np.ones((n, A_DIM - ROT_DIM), f32)
    zero8 = np.zeros((n, half), f32)
    zero = np.zeros((n, A_DIM - ROT_DIM), f32)
    c = np.concatenate([cos, cos, one], axis=1)
    s1 = np.concatenate([-sin, zero8, zero], axis=1)
    s2 = np.concatenate([zero8, sin, zero], axis=1)
    return tuple(jnp.asarray(np.tile(t, (1, LANES // A_DIM))) for t in (c, s1, s2))


def _project(x, pos_tables, ln_g, ln_b, w_perm, wgg, bgg, ts):
    bz, s, d = x.shape
    widths = (A_W, None, None, G_K, G_K, G_V, G_K, G_V, M_W)
    tok = lambda w: (pl.BlockSpec((1, ts, w), lambda j, b: (b, j, 0)) if w else
                     pl.BlockSpec((1, A_W, ts), lambda j, b: (b, 0, j)))
    full = lambda a: pl.BlockSpec(a.shape, lambda j, b: (0,) * a.ndim)
    tab = pl.BlockSpec((ts, LANES), lambda j, b: (j, 0))
    return pl.pallas_call(
        _proj_kernel,
        grid=(s // ts, bz),
        in_specs=[tok(d), full(ln_g), full(ln_b), full(w_perm), full(wgg), full(bgg), tab, tab, tab],
        out_specs=[tok(w) for w in widths],
        out_shape=[jax.ShapeDtypeStruct((bz, s, w) if w else (bz, A_W, s), F32) for w in widths],
        compiler_params=_cparams(("arbitrary", "arbitrary")),
        name="proj",
    )(x, ln_g, ln_b, w_perm, wgg, bgg, *pos_tables)


def _mem_kv_kernel(x_ref, w_ref, kt_ref, vt_ref):
    kv = jnp.dot(x_ref[0].astype(BF16), w_ref[...], preferred_element_type=F32)
    kt_ref[0] = kv[:, :M_W].T
    vt_ref[0] = kv[:, M_W:].T


def _mem_kv(mem, w_bf16):
    bz, m, d = mem.shape
    out = pl.BlockSpec((1, M_W, m), lambda b: (b, 0, 0))
    return pl.pallas_call(
        _mem_kv_kernel,
        grid=(bz,),
        in_specs=[pl.BlockSpec((1, m, d), lambda b: (b, 0, 0)), pl.BlockSpec(w_bf16.shape, lambda b: (0, 0))],
        out_specs=[out, out],
        out_shape=[jax.ShapeDtypeStruct((bz, M_W, m), F32)] * 2,
        compiler_params=_cparams(("arbitrary",)),
        name="mem_kv",
    )(mem, w_bf16)


_MOBA_HG = 4


_MOBA_VA = A_DIM + 16


def _moba_prompt_kernel(nblk, q_ref, kt_ref, vt_ref, o_ref, kb_scr, vt_scr, km_scr, selb_scr, acc_scr,
                        sa_scr, sb_scr, sd_scr):
    blk = MOBA_BLOCK
    hg = _MOBA_HG
    va = _MOBA_VA
    i = pl.program_id(2)

    @pl.when(i == 0)
    def _():
        lane = lax.broadcasted_iota(jnp.int32, (1, hg * A_DIM), 1)
        ones = jnp.ones((va - A_DIM, blk), BF16)
        for j in range(nblk):
            kj = kt_ref[0, :, j * blk:(j + 1) * blk].T
            kb_scr[j] = kj.astype(BF16)
            kmj = jnp.mean(kj, axis=0, keepdims=True)
            vj = vt_ref[0, :, j * blk:(j + 1) * blk].astype(BF16)
            for h in range(hg):
                km_scr[h * nblk + j:h * nblk + j + 1, :] = jnp.where(lane // A_DIM == h, kmj, 0.0)
                vt_scr[j, h * va:h * va + A_DIM, :] = vj[h * A_DIM:(h + 1) * A_DIM, :]
                vt_scr[j, h * va + A_DIM:(h + 1) * va, :] = ones

    q_t = q_ref[0].T
    row = lax.broadcasted_iota(jnp.int32, q_t.shape, 0)
    q_tb = [jnp.where((row >= A_DIM * h) & (row < A_DIM * (h + 1)), q_t, 0.0).astype(BF16) for h in range(hg)]
    km_hi, km_lo = _hi_lo(km_scr[...])
    q_hi, q_lo = _hi_lo(q_t)
    d3 = lambda a, w: jnp.dot(a, w, preferred_element_type=F32)
    g = (d3(km_hi, q_hi) + (d3(km_lo, q_hi) + d3(km_hi, q_lo))).reshape(hg, nblk, blk)
    blk_iota = lax.broadcasted_iota(jnp.int32, g.shape, 1)
    cnt = jnp.zeros(g.shape, jnp.int32)
    for jp in range(nblk):
        gj = g[:, jp:jp + 1, :]
        beats = (gj > g) | ((gj == g) & (jp < blk_iota))
        cnt = cnt + jnp.where(beats, 1, 0) * (jp < i).astype(jnp.int32)
    selb_scr[...] = jnp.where((blk_iota < i) & (cnt < MOBA_TOPK), 0.0, NEG)

    def scores(j, s_ref):
        kj = kb_scr[j]
        for h in range(hg):
            s_ref[h] = jnp.dot(kj, q_tb[h], preferred_element_type=F32)

    kpos = lax.broadcasted_iota(jnp.int32, (blk, blk), 0)
    qpos = lax.broadcasted_iota(jnp.int32, (blk, blk), 1)

    def absorb(j, s_ref, ms):
        vtj = vt_scr[j]
        new = []
        for h in range(hg):
            s = s_ref[h]
            if ms is None:
                s = jnp.where(kpos <= qpos, s, NEG)
                m_new = shift = jnp.max(s, axis=0, keepdims=True)
            else:
                bias = selb_scr[h, pl.ds(j, 1), :]
                m_new = jnp.maximum(ms[h], jnp.max(s, axis=0, keepdims=True) + bias)
                shift = m_new - bias
            pv = jnp.dot(vtj[h * va:(h + 1) * va, :], jnp.exp2(s - shift).astype(BF16), preferred_element_type=F32)
            acc_scr[h] = pv if ms is None else jnp.exp2(ms[h] - m_new) * acc_scr[h] + pv
            new.append(m_new)
        return tuple(new)

    last = nblk - 1
    scores(i, sd_scr)
    scores(0, sa_scr)
    ms = absorb(i, sd_scr, None)

    def past(jj, ms):
        ja, jb = 2 * jj, 2 * jj + 1
        scores(jb, sb_scr)
        ms = absorb(ja, sa_scr, ms)
        scores(jnp.minimum(ja + 2, last), sa_scr)
        return absorb(jb, sb_scr, ms)

    lax.fori_loop(0, (i + 1) // 2, past, ms)
    o_t = jnp.concatenate([acc_scr[h, :A_DIM, :] / acc_scr[h, A_DIM:A_DIM + 1, :] for h in range(hg)], axis=0)
    o_ref[0] = o_t.T


def _moba_prompt(q, kt, vt):
    bz, s, _ = q.shape
    blk = MOBA_BLOCK
    nblk = s // blk
    assert nblk % 2 == 0
    w = _MOBA_HG * A_DIM
    return pl.pallas_call(
        functools.partial(_moba_prompt_kernel, nblk),
        grid=(bz, A_W // w, nblk),
        in_specs=[pl.BlockSpec((1, blk, w), lambda b, hp, i: (b, i, hp)),
                  pl.BlockSpec((1, w, s), lambda b, hp, i: (b, hp, 0)),
                  pl.BlockSpec((1, w, s), lambda b, hp, i: (b, hp, 0))],
        out_specs=pl.BlockSpec((1, blk, w), lambda b, hp, i: (b, i, hp)),
        out_shape=jax.ShapeDtypeStruct((bz, s, A_W), F32),
        scratch_shapes=[pltpu.VMEM((nblk, blk, w), BF16), pltpu.VMEM((nblk, _MOBA_HG * _MOBA_VA, blk), BF16),
                        pltpu.VMEM((_MOBA_HG * nblk, w), F32), pltpu.VMEM((_MOBA_HG, nblk, blk), F32),
                        pltpu.VMEM((_MOBA_HG, _MOBA_VA, blk), F32)]
                       + [pltpu.VMEM((_MOBA_HG, blk, blk), F32)] * 3,
        compiler_params=_cparams(("arbitrary", "arbitrary", "arbitrary")),
        name="moba_prompt",
    )(q, kt, vt)


_PG_CHUNK = 8
_K_SLOTS = 8


def _moba_sample_kernel(t_new, n_pages, page, pt_ref, q_ref, kn_ref, vn_ref, ck_hbm, cv_hbm, o_ref,
                        kbuf, vbuf, ksem, vsem, fsem, s_scr, p_scr, bias_scr, flag_v, flag_s, unit_s):
    b = pl.program_id(0)
    nb = pl.num_programs(0)
    n_chunks = n_pages // _PG_CHUNK
    ppb = MOBA_BLOCK // page
    n_past = n_pages // ppb
    rows = t_new * A_HEADS

    def k_copy(bb, u, pg):
        slot = u % _K_SLOTS
        return pltpu.make_async_copy(ck_hbm.at[0, pt_ref[bb, u * _PG_CHUNK + pg]], kbuf.at[slot, pg], ksem.at[slot])

    def k_start(bb, u):
        for pg in range(_PG_CHUNK):
            k_copy(bb, u, pg).start()

    @pl.when(b == 0)
    def _():
        for u in range(_K_SLOTS):
            k_start(b, u)

        def zero(pg, c):
            vbuf[pg] = jnp.zeros(vbuf.shape[1:], F32)
            return c
        lax.fori_loop(0, n_pages, zero, 0)

    lane = lax.broadcasted_iota(jnp.int32, (A_HEADS, A_W), 1)
    sub = lax.broadcasted_iota(jnp.int32, (A_HEADS, A_W), 0)
    headmask = (lane // A_DIM == sub).astype(F32)
    q = q_ref[0]
    qbd = jnp.concatenate([jnp.broadcast_to(q[t:t + 1, :], (A_HEADS, A_W)) * headmask for t in range(t_new)],
                          axis=0)
    q_hi = qbd.astype(BF16)
    q_lo = (qbd - q_hi.astype(F32)).astype(BF16)
    q_hl = jnp.concatenate([q_hi, q_lo], axis=0)

    for u in range(n_chunks):
        for pg in range(_PG_CHUNK):
            k_copy(b, u, pg).wait()
        for pg in range(_PG_CHUNK):
            s2 = jnp.dot(q_hl, kbuf[u % _K_SLOTS, pg].astype(BF16), preferred_element_type=F32)
            s_scr[u * _PG_CHUNK + pg] = s2[:rows] + s2[rows:]
        ahead = u + _K_SLOTS
        if ahead < n_chunks:
            k_start(b, ahead)
        else:
            @pl.when(b + 1 < nb)
            def _():
                k_start(b + 1, ahead - n_chunks)

    col = lax.broadcasted_iota(jnp.int32, (rows, n_past), 1)
    g = jnp.zeros((rows, n_past), F32)
    for jb in range(n_past):
        blk_s = s_scr[jb * ppb]
        for e in range(1, ppb):
            blk_s = blk_s + s_scr[jb * ppb + e]
        g = jnp.where(col == jb, jnp.sum(blk_s, axis=1, keepdims=True), g)
    cnt = jnp.zeros((rows, n_past), jnp.int32)
    for jp in range(n_past):
        gj = g[:, jp:jp + 1]
        cnt = cnt + jnp.where((gj > g) | ((gj == g) & (jp < col)), 1, 0)
    sel = cnt < MOBA_TOPK

    picked = jnp.where(sel, 1, 0)
    head_any = picked[0:A_HEADS]
    for t in range(1, t_new):
        head_any = jnp.maximum(head_any, picked[t * A_HEADS:(t + 1) * A_HEADS])
    flag_v[...] = jnp.zeros(flag_v.shape, jnp.int32)
    flag_v[:, 0:n_past] = head_any
    flag_copy = pltpu.make_async_copy(flag_v, flag_s, fsem)
    flag_copy.start()
    flag_copy.wait()

    def collect(jb, n_units):
        for h in range(A_HEADS):
            unit_s[n_units] = jb * A_HEADS + h
            n_units = n_units + flag_s[h, jb]
        return n_units
    n_units = lax.fori_loop(0, n_past, collect, 0)

    def v_copies(idx):
        unit = unit_s[idx]
        jb = unit // A_HEADS
        r0 = pl.multiple_of((unit % A_HEADS) * A_DIM, A_DIM)
        return [pltpu.make_async_copy(cv_hbm.at[0, pt_ref[b, jb * ppb + e], pl.ds(r0, A_DIM), :],
                                      vbuf.at[jb * ppb + e, pl.ds(r0, A_DIM), :], vsem) for e in range(ppb)]

    def v_start(idx, c):
        for cp in v_copies(idx):
            cp.start(priority=1)
        return c
    lax.fori_loop(0, n_units, v_start, 0)


    selb = jnp.where(sel, 0.0, NEG)
    for jb in range(n_past):
        bias_scr[jb] = jnp.broadcast_to(selb[:, jb:jb + 1], (rows, page))
    kn, vn = kn_ref[0], vn_ref[0]
    r_t = lax.broadcasted_iota(jnp.int32, (rows, 1), 0) // A_HEADS
    s_own = [jnp.where(r_t >= t, jnp.sum(qbd * kn[t:t + 1, :], axis=1, keepdims=True), NEG) for t in range(t_new)]

    def smax(pg, mx):
        sb = s_scr[pg] + bias_scr[pg // ppb]
        s_scr[pg] = sb
        return jnp.maximum(mx, sb)
    mx = lax.fori_loop(0, n_pages, smax, jnp.full((rows, page), NEG, F32), unroll=4)
    m = jnp.max(mx, axis=1, keepdims=True)
    for t in range(t_new):
        m = jnp.maximum(m, s_own[t])

    def sexp(pg, ls):
        p = jnp.exp2(s_scr[pg] - m)
        p_scr[pg] = p.astype(BF16)
        return ls + p
    ls = lax.fori_loop(0, n_pages, sexp, jnp.zeros((rows, page), F32), unroll=4)
    l = jnp.sum(ls, axis=1, keepdims=True)
    acc = jnp.zeros((rows, A_W), F32)
    for t in range(t_new):
        p_t = jnp.exp2(s_own[t] - m)
        l = l + p_t
        acc = acc + p_t * vn[t:t + 1, :]

    def v_wait(idx, c):
        for cp in v_copies(idx):
            cp.wait()
        return c
    lax.fori_loop(0, n_units, v_wait, 0)

    def v_page(pg, a):
        return a + lax.dot_general(p_scr[pg], vbuf[pg].astype(BF16), NT, preferred_element_type=F32)
    acc = lax.fori_loop(0, n_pages, v_page, acc, unroll=8)

    o = acc / l
    out_rows = [jnp.sum(o[t * A_HEADS:(t + 1) * A_HEADS, :] * headmask, axis=0, keepdims=True)
                for t in range(t_new)]
    out_rows.append(jnp.zeros((o_ref.shape[1] - t_new, A_W), F32))
    o_ref[0] = jnp.concatenate(out_rows, axis=0)


def _moba_sample(q8, kn8, vn8, ck, cv, page_table, t_new):
    db, tp, _ = q8.shape
    n_pages = page_table.shape[1]
    page = ck.shape[3]
    assert (n_pages * page) % MOBA_BLOCK == 0, "past length must end on a MoBA block boundary"
    assert MOBA_BLOCK % page == 0 and n_pages % (_PG_CHUNK * _K_SLOTS) == 0
    n_past = n_pages * page // MOBA_BLOCK
    assert MOBA_TOPK <= n_past <= LANES
    rows = t_new * A_HEADS
    tok = pl.BlockSpec((1, tp, A_W), lambda b, pt: (b, 0, 0))
    any_spec = pl.BlockSpec(memory_space=pl.ANY)
    grid_spec = pltpu.PrefetchScalarGridSpec(
        num_scalar_prefetch=1,
        grid=(db,),
        in_specs=[tok, tok, tok, any_spec, any_spec],
        out_specs=tok,
        scratch_shapes=[pltpu.VMEM((_K_SLOTS, _PG_CHUNK, A_W, page), F32), pltpu.VMEM((n_pages, A_W, page), F32),
                        pltpu.SemaphoreType.DMA((_K_SLOTS,)), pltpu.SemaphoreType.DMA(()), pltpu.SemaphoreType.DMA(()),
                        pltpu.VMEM((n_pages, rows, page), F32), pltpu.VMEM((n_pages, rows, page), BF16),
                        pltpu.VMEM((n_past, rows, page), F32),
                        pltpu.VMEM((A_HEADS, LANES), jnp.int32), pltpu.SMEM((A_HEADS, LANES), jnp.int32),
                        pltpu.SMEM((n_past * A_HEADS,), jnp.int32)],
    )
    return pl.pallas_call(
        functools.partial(_moba_sample_kernel, t_new, n_pages, page),
        grid_spec=grid_spec,
        out_shape=jax.ShapeDtypeStruct((db, tp, A_W), F32),
        compiler_params=_cparams(("arbitrary",)),
        name="moba_sample",
    )(page_table, q8, kn8, vn8, ck, cv)


def _gla_kernel(c, nch, carry, g, q_ref, k_ref, la_ref, v_ref, rg_ref, s0_ref, gg_ref, l_ref, e_ref, hm_ref,
                bd_ref, hv_ref, o_ref, sf_ref, st_scr, b_scr, qs_scr, k_scr, qb_scr, dch_scr, oo_scr):
    j = pl.program_id(1)
    if carry:
        @pl.when(j == 0)
        def _():
            st_scr[...] = s0_ref[...]

    kdec, v_t = [], []
    for gi in range(g):
        la = la_ref[gi]
        b = _dot_exact_lhs(l_ref[...], la)
        tot = _dot_exact_lhs(e_ref[...], la)
        qs = q_ref[gi] * (G_DK ** -0.5)
        k = k_ref[gi]
        b_scr[gi] = b
        qs_scr[gi] = qs
        k_scr[gi] = k
        qb_scr[gi] = qs * jnp.exp(b)
        dch_scr[gi] = jnp.exp(tot)
        kdec.append((k * jnp.exp(tot - b)).astype(BF16))
        v_t.append(v_ref[gi].T)
    lane_chunk = lax.broadcasted_iota(jnp.int32, v_t[0].shape, 1) // c
    jrow = lax.broadcasted_iota(jnp.int32, (c, G_K), 0)

    def chunk(ci, sts):
        r0 = pl.multiple_of(ci * c, c)
        new = []
        for gi in range(g):
            st = sts[gi] if carry else s0_ref[ci]
            o_inter = lax.dot_general(qb_scr[gi, pl.ds(r0, c), :].astype(BF16), st.astype(BF16), NT,
                                      preferred_element_type=F32)
            b_c, k_c, qs_c = (r[gi, pl.ds(r0, c), :] for r in (b_scr, k_scr, qs_scr))
            terms = []
            for i in range(c):
                d = jnp.where(jrow <= i, b_c[i:i + 1, :] - b_c, NEG)
                terms.append(qs_c[i:i + 1, :] * k_c * jnp.exp(d))
            t_all = jnp.concatenate(terms, axis=0).astype(BF16)
            att = jnp.dot(t_all, hm_ref[...], preferred_element_type=F32)
            v_c = v_ref[gi, pl.ds(r0, c), :]
            o_intra = jnp.sum(att.reshape(c, c, G_V) * v_c[None, :, :], axis=1)
            oo_scr[gi, pl.ds(r0, c), :] = o_inter + o_intra
            v_m = jnp.where(lane_chunk == ci, v_t[gi], 0.0).astype(BF16)
            upd = jnp.dot(v_m, kdec[gi], preferred_element_type=F32) * bd_ref[...]
            st_new = st * dch_scr[gi, pl.ds(r0, 1), :] + upd
            if not carry:
                sf_ref[ci] = st_new
            new.append(st_new)
        return tuple(new)

    init = tuple(st_scr[gi] if carry else jnp.zeros(st_scr.shape[1:], F32) for gi in range(g))
    sts = lax.fori_loop(0, nch, chunk, init)
    if carry:
        for gi in range(g):
            st_scr[gi] = sts[gi]

        @pl.when(j == pl.num_programs(1) - 1)
        def _():
            for gi in range(g):
                sf_ref[gi] = sts[gi]

    hv = hv_ref[...]
    for gi in range(g):
        o = oo_scr[gi]
        sq_hi, sq_lo = _hi_lo(o * o)
        ms = jnp.dot(sq_hi, hv, preferred_element_type=F32) + jnp.dot(sq_lo, hv, preferred_element_type=F32)
        rg = rg_ref[gi]
        o_ref[gi] = o * lax.rsqrt(ms + RMS_EPS) * gg_ref[...] * (rg / (1.0 + jnp.exp(-rg)))


def _gla(gq, gk, la, gv, rg, s0_t, gg, c, nch, carry):
    bz, t, _ = gq.shape
    tt = c * nch
    idx = jnp.arange(tt)
    same = (idx[:, None] // c) == (idx[None, :] // c)
    l_mat = (same & (idx[None, :] <= idx[:, None])).astype(BF16)
    e_mat = same.astype(BF16)
    hm = ((jnp.arange(G_K)[:, None] // G_DK) == (jnp.arange(G_V)[None, :] // G_DV)).astype(BF16)
    bd = ((jnp.arange(G_V)[:, None] // G_DV) == (jnp.arange(G_K)[None, :] // G_DK)).astype(F32)
    hv = (((jnp.arange(G_V)[:, None] // G_DV) == (jnp.arange(G_V)[None, :] // G_DV)).astype(F32) / G_DV).astype(BF16)
    assert G_DV & (G_DV - 1) == 0, "1 / G_DV must be exact in bf16"
    g = next(n for n in (4, 2, 1) if bz % n == 0) if carry else 1
    tok = lambda w: pl.BlockSpec((g, tt, w), lambda b, j: (b, j, 0))
    full = lambda a: pl.BlockSpec(a.shape, lambda b, j: (0,) * a.ndim)
    ns = g if carry else nch
    st_spec = pl.BlockSpec((ns, G_V, G_K), lambda b, j: (b, 0, 0))
    return pl.pallas_call(
        functools.partial(_gla_kernel, c, nch, carry, g),
        grid=(bz // g, t // tt),
        in_specs=[tok(G_K), tok(G_K), tok(G_K), tok(G_V), tok(G_V), st_spec, full(gg), full(l_mat), full(e_mat),
                  full(hm), full(bd), full(hv)],
        out_specs=[tok(G_V), st_spec],
        out_shape=[jax.ShapeDtypeStruct((bz, t, G_V), F32), jax.ShapeDtypeStruct(s0_t.shape, F32)],
        scratch_shapes=[pltpu.VMEM((g, G_V, G_K), F32)] + [pltpu.VMEM((g, tt, G_K), F32)] * 5
                       + [pltpu.VMEM((g, tt, G_V), F32)],
        compiler_params=_cparams(("arbitrary", "arbitrary")),
        name="gla",
    )(gq, gk, la, gv, rg, s0_t, gg, l_mat, e_mat, hm, bd, hv)


def _state_to_t(s):
    n = s.shape[0]
    eye = jnp.eye(G_HEADS, dtype=s.dtype)
    return jnp.einsum("nhdv,hg->nhvgd", s, eye).reshape(n, G_V, G_K)


def _state_from_t(st):
    n = st.shape[0]
    s5 = st.reshape(n, G_HEADS, G_DV, G_HEADS, G_DK)
    return jnp.stack([s5[:, h, :, h, :] for h in range(G_HEADS)], axis=1).transpose(0, 1, 3, 2)


def _mem_attn_kernel(q_ref, mkt_ref, mvt_ref, o_ref):
    q = q_ref[0]
    mkt = mkt_ref[0].astype(BF16)
    mvt = mvt_ref[0].astype(BF16)
    lane = lax.broadcasted_iota(jnp.int32, q.shape, 1)
    out = jnp.zeros(q.shape, F32)
    for h in range(M_HEADS):
        hsel = (lane // M_DIM) == h
        s = jnp.dot(jnp.where(hsel, q, 0.0).astype(BF16), mkt, preferred_element_type=F32)
        m = jnp.max(s, axis=1, keepdims=True)
        p = jnp.exp(s - m)
        l = jnp.sum(p, axis=1, keepdims=True)
        o_h = lax.dot_general(p.astype(BF16), mvt, NT, preferred_element_type=F32)
        out = out + jnp.where(hsel, o_h / l, 0.0)
    o_ref[0] = out


def _mem_attn(qm, mkt, mvt, tm):
    bz, t, _ = qm.shape
    mem = mkt.shape[2]
    return pl.pallas_call(
        _mem_attn_kernel,
        grid=(bz, t // tm),
        in_specs=[pl.BlockSpec((1, tm, M_W), lambda b, j: (b, j, 0)),
                  pl.BlockSpec((1, M_W, mem), lambda b, j: (b, 0, 0)),
                  pl.BlockSpec((1, M_W, mem), lambda b, j: (b, 0, 0))],
        out_specs=pl.BlockSpec((1, tm, M_W), lambda b, j: (b, j, 0)),
        out_shape=jax.ShapeDtypeStruct((bz, t, M_W), F32),
        compiler_params=_cparams(("arbitrary", "arbitrary")),
        name="mem_attn",
    )(qm, mkt, mvt)


def _mix_out_kernel(x_ref, oa_ref, og_ref, om_ref, lng_ref, lnb_ref, wo_ref, g1_ref, b1_ref, wr_ref, br_ref,
                    tri_ref, cnt0_ref, h1_all, rt_all, h1_ref, rt_ref, cnt_ref, carry_scr):
    del h1_all, rt_all

    @pl.when(pl.program_id(0) == 0)
    def _():
        carry_scr[...] = cnt0_ref[...]

    h = _layer_norm(x_ref[...], lng_ref[...], lnb_ref[...])
    o = jnp.dot(oa_ref[...].astype(BF16), wo_ref[0:A_W, :], preferred_element_type=F32)
    o = o + jnp.dot(og_ref[...].astype(BF16), wo_ref[A_W:A_W + G_V, :], preferred_element_type=F32)
    o = o + jnp.dot(om_ref[...].astype(BF16), wo_ref[A_W + G_V:, :], preferred_element_type=F32)
    h1 = _layer_norm(DEEPNORM_ALPHA * h + o, g1_ref[...], b1_ref[...])
    h1_ref[...] = h1
    h1_hi = h1.astype(BF16)
    h1_lo = (h1 - h1_hi.astype(F32)).astype(BF16)
    mm = lambda a, w: jnp.dot(a, w, preferred_element_type=F32)
    work = mm(h1_hi, wr_ref[0]) + (mm(h1_lo, wr_ref[0]) + mm(h1_hi, wr_ref[1])) + br_ref[...]
    lane = lax.broadcasted_iota(jnp.int32, work.shape, 1).astype(F32)
    onehot = jnp.zeros(work.shape, F32)
    vals, eids = [], []
    for _ in range(TOP_K):
        mk = jnp.max(work, axis=1, keepdims=True)
        ek = jnp.min(jnp.where(work == mk, lane, float(LANES)), axis=1, keepdims=True)
        hit = lane == ek
        onehot = jnp.where(hit, 1.0, onehot)
        work = jnp.where(hit, -jnp.inf, work)
        vals.append(mk)
        eids.append(ek)
    ex = [jnp.exp(v - vals[0]) for v in vals]
    den = ex[0] + ex[1] + ex[2] + ex[3]
    pos_full = jnp.dot(tri_ref[...], onehot.astype(BF16), preferred_element_type=F32) + carry_scr[...]
    carry_scr[...] = carry_scr[...] + jnp.sum(onehot, axis=0, keepdims=True)
    cnt_ref[...] = carry_scr[...]
    rt = jnp.zeros(work.shape, F32)
    for k in range(TOP_K):
        pk = jnp.sum(jnp.where(lane == eids[k], pos_full, 0.0), axis=1, keepdims=True)
        rt = jnp.where(lane == RT_GATE + k, ex[k] / den, rt)
        rt = jnp.where(lane == RT_EID + k, eids[k], rt)
        rt = jnp.where(lane == RT_POS + k, pk, rt)
    rt_ref[...] = rt


def _mix_out(x, oa, og, om, ln_g, ln_b, wo_bf16, g1, b1, wr_pad, br_pad, cnt0, h1_all, rt_all, row0, tm):
    n, d = x.shape
    off = row0 // tm
    assert row0 % tm == 0 and n % tm == 0
    tri = (jnp.arange(tm)[None, :] < jnp.arange(tm)[:, None]).astype(BF16)
    tok = lambda w: pl.BlockSpec((tm, w), lambda i: (i, 0))
    out = lambda w: pl.BlockSpec((tm, w), lambda i: (i + off, 0))
    full = lambda a: pl.BlockSpec(a.shape, lambda i: (0,) * a.ndim)
    any_spec = pl.BlockSpec(memory_space=pl.ANY)
    return pl.pallas_call(
        _mix_out_kernel,
        grid=(n // tm,),
        in_specs=[tok(d), tok(A_W), tok(G_V), tok(M_W), full(ln_g), full(ln_b), full(wo_bf16), full(g1), full(b1),
                  full(wr_pad), full(br_pad), full(tri), full(cnt0), any_spec, any_spec],
        out_specs=[out(d), out(LANES), pl.BlockSpec((1, LANES), lambda i: (0, 0))],
        out_shape=[jax.ShapeDtypeStruct(h1_all.shape, F32), jax.ShapeDtypeStruct(rt_all.shape, F32),
                   jax.ShapeDtypeStruct((1, LANES), F32)],
        input_output_aliases={13: 0, 14: 1},
        scratch_shapes=[pltpu.VMEM((1, LANES), F32)],
        compiler_params=_cparams(("arbitrary",)),
        name="mix_out",
    )(x, oa, og, om, ln_g, ln_b, wo_bf16, g1, b1, wr_pad, br_pad, tri, cnt0, h1_all, rt_all)


def _row_copy(src, s_row, dst, d_row, sem):
    return pltpu.make_async_copy(src.at[pl.ds(s_row, 1)], dst.at[pl.ds(d_row, 1)], sem)


def _dispatch_kernel(zs_ref, ze_ref, dest_ref, h1_ref, xs_hbm, zero_scr, sems):
    tm = h1_ref.shape[0]
    sem, zsem = sems.at[0], sems.at[1]

    def issue(r, c):
        for k in range(TOP_K):
            _row_copy(h1_ref, r, xs_hbm, dest_ref[r * TOP_K + k], sem).start()
        return c
    lax.fori_loop(0, tm, issue, 0, unroll=4)

    @pl.when(pl.program_id(0) == pl.num_programs(0) - 1)
    def _():
        zero_scr[...] = jnp.zeros(zero_scr.shape, F32)

        def zero_rows(wait):
            def per_row(r, c):
                cp = _row_copy(zero_scr, 0, xs_hbm, r, zsem)
                if wait:
                    cp.wait()
                else:
                    cp.start()
                return c

            def per_expert(e, c):
                return lax.fori_loop(zs_ref[e], ze_ref[e], per_row, c)
            lax.fori_loop(0, N_EXPERTS, per_expert, 0)
        zero_rows(False)
        zero_rows(True)

    def drain(r, c):
        for k in range(TOP_K):
            _row_copy(h1_ref, r, xs_hbm, 0, sem).wait()
        return c
    lax.fori_loop(0, tm, drain, 0, unroll=4)


def _dispatch(h1, dest_flat, z_start, z_end, n_rows, tm):
    n, d = h1.shape
    grid_spec = pltpu.PrefetchScalarGridSpec(
        num_scalar_prefetch=2,
        grid=(n // tm,),
        in_specs=[pl.BlockSpec((tm * TOP_K,), lambda i, zs, ze: (i,), memory_space=pltpu.SMEM),
                  pl.BlockSpec((tm, d), lambda i, zs, ze: (i, 0))],
        out_specs=pl.BlockSpec(memory_space=pl.ANY),
        scratch_shapes=[pltpu.VMEM((8, d), F32), pltpu.SemaphoreType.DMA((2,))],
    )
    return pl.pallas_call(
        _dispatch_kernel,
        grid_spec=grid_spec,
        out_shape=jax.ShapeDtypeStruct((n_rows, d), F32),
        compiler_params=_cparams(("arbitrary",)),
        name="moe_dispatch",
    )(z_start, z_end, dest_flat, h1)


_FF_CHUNK = 512


def _moe_kernel(be_ref, nu_ref, x_ref, wg_ref, bg_ref, wu_ref, bu_ref, wd_ref, bd_ref, y_ref,
                wgb_scr, wub_scr, wdb_scr):
    i = pl.program_id(0)
    prev = be_ref[jnp.maximum(i - 1, 0)]

    @pl.when((i == 0) | (be_ref[i] != prev))
    def _():
        wgb_scr[...] = wg_ref[0].astype(BF16)
        wub_scr[...] = wu_ref[0].astype(BF16)
        wdb_scr[...] = wd_ref[0].astype(BF16)

    @pl.when(i < nu_ref[0])
    def _():
        x = x_ref[...].astype(BF16)
        d_ff = wgb_scr.shape[1]
        y = jnp.zeros(y_ref.shape, F32) + bd_ref[0]
        for f0 in range(0, d_ff, _FF_CHUNK):
            f1 = f0 + _FF_CHUNK
            g = jnp.dot(x, wgb_scr[:, f0:f1], preferred_element_type=F32) + bg_ref[0, :, f0:f1]
            u = jnp.dot(x, wub_scr[:, f0:f1], preferred_element_type=F32) + bu_ref[0, :, f0:f1]
            g = jnp.minimum(g, SWIGLU_LIMIT)
            u = jnp.clip(u, -SWIGLU_LIMIT, SWIGLU_LIMIT)
            act = g * (1.0 / (1.0 + jnp.exp(-SWIGLU_ALPHA * g))) * (u + 1.0)
            y = y + jnp.dot(act.astype(BF16), wdb_scr[f0:f1, :], preferred_element_type=F32)
        y_ref[...] = y

    @pl.when(i >= nu_ref[0])
    def _():
        y_ref[...] = jnp.zeros(y_ref.shape, F32)


def _moe_experts(xs, block_e, n_used, wg, bg, wu, bu, wd, bd, tm):
    rows, d = xs.shape
    n_blocks = rows // tm
    d_ff = wg.shape[2]
    wspec = lambda a: pl.BlockSpec((1,) + a.shape[1:], lambda i, be, nu: (be[i], 0, 0))
    grid_spec = pltpu.PrefetchScalarGridSpec(
        num_scalar_prefetch=2,
        grid=(n_blocks,),
        in_specs=[pl.BlockSpec((tm, d), lambda i, be, nu: (jnp.minimum(i, nu[0] - 1), 0)),
                  wspec(wg), wspec(bg), wspec(wu), wspec(bu), wspec(wd), wspec(bd)],
        out_specs=pl.BlockSpec((tm, d), lambda i, be, nu: (i, 0)),
        scratch_shapes=[pltpu.VMEM((d, d_ff), BF16), pltpu.VMEM((d, d_ff), BF16), pltpu.VMEM((d_ff, d), BF16)],
    )
    return pl.pallas_call(
        _moe_kernel,
        grid_spec=grid_spec,
        out_shape=jax.ShapeDtypeStruct((rows, d), F32),
        compiler_params=_cparams(("arbitrary",)),
        name="moe_experts",
    )(block_e, n_used, xs, wg, bg, wu, bu, wd, bd)


def _combine_kernel(nt_a, dcur_ref, dnxt_ref, h1_ref, rt_ref, g_ref, b_ref, yb_hbm, oa_ref, ob_ref, gbuf, sem):
    i = pl.program_id(0)
    tm = h1_ref.shape[0]

    def gather(dref, slot, wait):
        def body(r, c):
            for k in range(TOP_K):
                cp = pltpu.make_async_copy(yb_hbm.at[pl.ds(dref[r * TOP_K + k], 1)],
                                           gbuf.at[slot, k, pl.ds(r, 1)], sem.at[slot])
                if wait:
                    cp.wait()
                else:
                    cp.start()
            return c
        lax.fori_loop(0, tm, body, 0, unroll=4)

    @pl.when(i == 0)
    def _():
        gather(dcur_ref, 0, False)

    @pl.when(i + 1 < pl.num_programs(0))
    def _():
        gather(dnxt_ref, (i + 1) % 2, False)

    slot = i % 2
    gather(dcur_ref, slot, True)
    rt = rt_ref[...]
    f = rt[:, RT_GATE:RT_GATE + 1] * gbuf[slot, 0]
    for k in range(1, TOP_K):
        f = f + rt[:, RT_GATE + k:RT_GATE + k + 1] * gbuf[slot, k]
    y = _layer_norm(DEEPNORM_ALPHA * h1_ref[...] + f, g_ref[...], b_ref[...])

    @pl.when(i < nt_a)
    def _():
        oa_ref[...] = y

    @pl.when(i >= nt_a)
    def _():
        ob_ref[...] = y


def _combine_ln2(h1, rt, dest_flat, yb, g, b, n_a, tm):
    n, d = h1.shape
    nt = n // tm
    nt_a = n_a // tm
    assert n_a % tm == 0 and 0 < nt_a < nt
    tok = lambda w: pl.BlockSpec((tm, w), lambda i: (i, 0))
    full = lambda a: pl.BlockSpec(a.shape, lambda i: (0,) * a.ndim)
    return pl.pallas_call(
        functools.partial(_combine_kernel, nt_a),
        grid=(nt,),
        in_specs=[pl.BlockSpec((tm * TOP_K,), lambda i: (i,), memory_space=pltpu.SMEM),
                  pl.BlockSpec((tm * TOP_K,), lambda i: (jnp.minimum(i + 1, nt - 1),), memory_space=pltpu.SMEM),
                  tok(d), tok(LANES), full(g), full(b), pl.BlockSpec(memory_space=pl.ANY)],
        out_specs=[pl.BlockSpec((tm, d), lambda i: (jnp.minimum(i, nt_a - 1), 0)),
                   pl.BlockSpec((tm, d), lambda i: (jnp.maximum(i - nt_a, 0), 0))],
        out_shape=[jax.ShapeDtypeStruct((n_a, d), F32), jax.ShapeDtypeStruct((n - n_a, d), F32)],
        scratch_shapes=[pltpu.VMEM((2, TOP_K, tm, d), F32), pltpu.SemaphoreType.DMA((2,))],
        compiler_params=_cparams(("arbitrary",)),
        name="moe_combine_ln2",
    )(dest_flat, dest_flat, h1, rt, g, b, yb)


def _layout(rt, counts, n, tm):
    counts = counts.astype(jnp.int32)
    padded = (counts + tm - 1) // tm * tm
    pad_end = jnp.cumsum(padded)
    pad_start = pad_end - padded
    eid = rt[:, RT_EID:RT_EID + TOP_K].astype(jnp.int32)
    pos = rt[:, RT_POS:RT_POS + TOP_K].astype(jnp.int32)
    first = jnp.sum(jnp.where(eid[..., None] == jnp.arange(N_EXPERTS), pad_start, 0), axis=-1)
    dest = (first + pos).reshape(-1)
    n_blocks = -(-(n * TOP_K) // tm) + N_EXPERTS
    first_row = jnp.arange(n_blocks, dtype=jnp.int32) * tm
    block_e = jnp.minimum(jnp.sum(pad_end[None, :] <= first_row[:, None], axis=1), N_EXPERTS - 1).astype(jnp.int32)
    n_used = (pad_end[-1] // tm).astype(jnp.int32).reshape(1)
    return dest, block_e, n_used, (pad_start + counts).astype(jnp.int32), pad_end.astype(jnp.int32), n_blocks * tm


def _hd_rows(a):
    nd = a.ndim
    t = jnp.transpose(a, tuple(range(nd - 3)) + (nd - 2, nd - 1, nd - 3))
    return t.reshape(t.shape[:-3] + (t.shape[-3] * t.shape[-2], t.shape[-1]))


def _rows_hd(at, heads):
    t = at.reshape(at.shape[:-2] + (heads, at.shape[-2] // heads, at.shape[-1]))
    nd = t.ndim
    return jnp.transpose(t, tuple(range(nd - 3)) + (nd - 1, nd - 3, nd - 2))


def kernel(x_prompt, x_sample, cache_k, cache_v, page_table, state_gla, cache_mem_k, cache_mem_v, mem_prompt,
           ln_in_g, ln_in_b, w_in, w_gla_gate, b_gla_gate, g_gla, w_mem_kv, w_out, ln1_g, ln1_b,
           w_router, b_router, w_gate, b_gate, w_up, b_up, w_down, b_down, ln2_g, ln2_b):
    assert w_in.shape[0] == DEPTH == 1
    bz, seq, d = x_prompt.shape
    db, t_new, _ = x_sample.shape
    page = cache_k.shape[2]
    past = page_table.shape[1] * page
    mem_len = mem_prompt.shape[1]
    n_p, n_s = bz * seq, db * t_new
    n = n_p + n_s
    tm = TOK_TM
    t_pad = 8
    row = lambda a: a.reshape(1, -1)

    wi = w_in[0]
    c_lg = 3 * A_W + 2 * G_K + G_V
    w_perm = jnp.concatenate([wi[:, :c_lg], wi[:, c_lg + G_LOWRANK:], wi[:, c_lg:c_lg + G_LOWRANK],
                              jnp.zeros((d, LANES - G_LOWRANK), F32)], axis=1).astype(BF16)
    wgg = jnp.stack(_hi_lo(jnp.zeros((LANES, G_K), F32).at[:G_LOWRANK].set(w_gla_gate[0])))
    bgg = row(b_gla_gate[0])
    gg = row(jnp.tile(g_gla[0], G_HEADS))
    wo = w_out[0].astype(BF16)
    wr32 = jnp.zeros((d, LANES), F32).at[:, :N_EXPERTS].set(w_router[0])
    wr_hi = wr32.astype(BF16)
    wr = jnp.stack([wr_hi, (wr32 - wr_hi.astype(F32)).astype(BF16)])
    br = jnp.full((1, LANES), -jnp.inf, F32).at[0, :N_EXPERTS].set(b_router[0])
    ln_g, ln_b = row(ln_in_g), row(ln_in_b)

    tabs_p = _rope_tables(np.arange(seq))
    tabs_s = _rope_tables(np.tile(past + np.arange(t_new), db))
    qp, ktp, vtp, gqp, gkp, gvp, lap, rgp, qmp = _project(x_prompt, tabs_p, ln_g, ln_b, w_perm, wgg, bgg, tm)
    xs3 = x_sample.reshape(1, n_s, d)
    qs, kts, vts, gqs, gks, gvs, las, rgs, qms = _project(xs3, tabs_s, ln_g, ln_b, w_perm, wgg, bgg, n_s)
    ks, vs = kts[0].T, vts[0].T
    mkt_p, mvt_p = _mem_kv(mem_prompt, w_mem_kv[0].astype(BF16))

    oa_p = _moba_prompt(qp, ktp, vtp)
    s0_p = jnp.zeros((bz, G_V, G_K), F32)
    og_p, st_p = _gla(gqp, gkp, lap, gvp, rgp, s0_p, gg, GLA_CHUNK, MOBA_BLOCK // GLA_CHUNK, True)
    om_p = _mem_attn(qmp, mkt_p, mvt_p, tm)

    pad_t = lambda a: jnp.pad(a.reshape(db, t_new, -1), ((0, 0), (0, t_pad - t_new), (0, 0)))
    oa_s = _moba_sample(pad_t(qs), pad_t(ks), pad_t(vs), _hd_rows(cache_k), _hd_rows(cache_v), page_table,
                        t_new)[:, :t_new]
    seq_per_step = 16
    grp = lambda a: pad_t(a).reshape(db // seq_per_step, seq_per_step * t_pad, -1)
    og_s, st_s = _gla(grp(gqs), grp(gks), grp(las), grp(gvs), grp(rgs), _state_to_t(state_gla[0]), gg,
                      t_pad, seq_per_step, False)
    og_s = og_s.reshape(db, t_pad, G_V)[:, :t_new]
    om_s = _mem_attn(pad_t(qms), _hd_rows(cache_mem_k[0]), _hd_rows(cache_mem_v[0]), t_pad)[:, :t_new]

    mix_w = (ln_g, ln_b, wo, row(ln1_g[0]), row(ln1_b[0]), wr, br)
    flat = lambda a, rows: a.reshape(rows, -1)
    h1, rt, cnt = _mix_out(flat(x_prompt, n_p), flat(oa_p, n_p), flat(og_p, n_p), flat(om_p, n_p), *mix_w,
                           jnp.zeros((1, LANES), F32), jnp.zeros((n, d), F32), jnp.zeros((n, LANES), F32), 0, tm)
    h1, rt, cnt = _mix_out(flat(x_sample, n_s), flat(oa_s, n_s), flat(og_s, n_s), flat(om_s, n_s), *mix_w,
                           cnt, h1, rt, n_p, tm)

    dest, block_e, n_used, z_start, z_end, n_rows = _layout(rt, cnt[0, :N_EXPERTS], n, tm)
    xs_sorted = _dispatch(h1, dest, z_start, z_end, n_rows, tm)
    b3 = lambda a: a[0].reshape(N_EXPERTS, 1, -1)
    yb = _moe_experts(xs_sorted, block_e, n_used, w_gate[0], b3(b_gate), w_up[0], b3(b_up), w_down[0], b3(b_down), tm)
    y_p, y_s = _combine_ln2(h1, rt, dest, yb, row(ln2_g[0]), row(ln2_b[0]), n_p, tm)

    return (y_p.reshape(bz, seq, d), y_s.reshape(db, t_new, d),
            _rows_hd(ktp, A_HEADS)[None], _rows_hd(vtp, A_HEADS)[None],
            _state_from_t(st_p)[None],
            _rows_hd(mkt_p, M_HEADS)[None], _rows_hd(mvt_p, M_HEADS)[None],
            ks.reshape(1, db, t_new, A_HEADS, A_DIM), vs.reshape(1, db, t_new, A_HEADS, A_DIM),
            _state_from_t(st_s)[None])
```
